```python
import math
import jax, jax.numpy as jnp
from jax import lax
import numpy as np

D_MODEL = 1024
BATCH = 4
SEQ = 4096
DEPTH = 2

H_A = 8
DQK_A = D_MODEL // 16
DV_A = D_MODEL // 8
QK_A = H_A * DQK_A
V_A = H_A * DV_A
CHUNK = 64
CONV_K = 4
H_B = 8
HD_B = D_MODEL // 8
W_B = H_B * HD_B
QBLK = 128
D_FF = 4 * D_MODEL
EPS = 1e-6
IN_SIZES = (QK_A, QK_A, V_A, V_A, H_A, H_A, W_B, W_B, W_B, D_MODEL, D_MODEL)
IN_COLS = sum(IN_SIZES)

kernel_name = "hybrid_mlstm_stickbreaking_block"


def rms_norm(x, g):
    xf = x.astype(jnp.float32)
    y = xf * lax.rsqrt(jnp.mean(xf * xf, axis=-1, keepdims=True) + EPS)
    return (y * g.astype(jnp.float32)).astype(x.dtype)


def head_rms_norm(x, g):
    xf = x.astype(jnp.float32)
    y = xf * lax.rsqrt(jnp.mean(xf * xf, axis=-1, keepdims=True) + EPS)
    return y * g.astype(jnp.float32)


def causal_conv(x, w):
    S = x.shape[1]
    xp = jnp.pad(x, ((0, 0), (CONV_K - 1, 0), (0, 0)))
    y = xp[:, 0:S] * w[0]
    for j in range(1, CONV_K):
        y = y + xp[:, j:j + S] * w[j]
    return y


def mlstm_chunkwise(q, k, v, i_pre, f_pre):
    B, S, H, _ = q.shape
    NC = S // CHUNK
    to_chunks = lambda a: jnp.moveaxis(a.astype(jnp.float32).reshape(B, NC, CHUNK, H, -1), 3, 1)
    q = to_chunks(q)
    k = to_chunks(k) * (DQK_A ** -0.5)
    v = to_chunks(v)
    gate = lambda a: jnp.moveaxis(a.astype(jnp.float32).reshape(B, NC, CHUNK, H), 3, 1)
    logf = jax.nn.log_sigmoid(gate(f_pre))
    ig = gate(i_pre)
    b = jnp.cumsum(logf, axis=-1)
    b_last = b[..., -1]

    a = b_last[..., None] - b + ig
    m_loc = jnp.max(a, axis=-1)
    w = jnp.exp(a - m_loc[..., None])
    S_loc = jnp.einsum('bhcs,bhcsk,bhcsv->bhckv', w, k, v)
    n_loc = jnp.einsum('bhcs,bhcsk->bhck', w, k)

    def step(carry, inp):
        S_prev, n_prev, m_prev = carry
        S_l, n_l, m_l, bl = inp
        m_new = jnp.maximum(bl + m_prev, m_l)
        sp = jnp.exp(bl + m_prev - m_new)
        sl = jnp.exp(m_l - m_new)
        S_new = sp[..., None, None] * S_prev + sl[..., None, None] * S_l
        n_new = sp[..., None] * n_prev + sl[..., None] * n_l
        return (S_new, n_new, m_new), (S_prev, n_prev, m_prev)

    init = (jnp.zeros_like(S_loc[:, :, 0]), jnp.zeros_like(n_loc[:, :, 0]), jnp.zeros_like(m_loc[:, :, 0]))
    xs = (jnp.moveaxis(S_loc, 2, 0), jnp.moveaxis(n_loc, 2, 0), jnp.moveaxis(m_loc, 2, 0), jnp.moveaxis(b_last, 2, 0))
    _, (S0, n0, m0) = lax.scan(step, init, xs)
    S0 = jnp.moveaxis(S0, 0, 2)
    n0 = jnp.moveaxis(n0, 0, 2)
    m0 = jnp.moveaxis(m0, 0, 2)

    causal = jnp.tril(jnp.ones((CHUNK, CHUNK), dtype=bool))
    D = jnp.where(causal, b[..., :, None] - b[..., None, :] + ig[..., None, :], -jnp.inf)
    b_inter = b + m0[..., None]
    m_t = jnp.maximum(b_inter, jnp.max(D, axis=-1))
    P = jnp.exp(D - m_t[..., None]) * jnp.einsum('bhctk,bhcsk->bhcts', q, k)
    inter = jnp.exp(b_inter - m_t)
    num = jnp.einsum('bhcts,bhcsv->bhctv', P, v) + inter[..., None] * jnp.einsum('bhctk,bhckv->bhctv', q, S0)
    den = jnp.sum(P, axis=-1) + inter * jnp.einsum('bhctk,bhck->bhct', q, n0)
    h = num / jnp.maximum(jnp.abs(den), jnp.exp(-m_t))[..., None]
    return jnp.moveaxis(h, 1, 3).reshape(B, S, H, DV_A)


def stick_breaking(q, k, v):
    S = q.shape[1]
    scale = HD_B ** -0.5
    outs = []
    for blk in range(S // QBLK):
        t0, t1 = blk * QBLK, (blk + 1) * QBLK
        z = jnp.einsum('bqhd,bshd->bhqs', q[:, t0:t1], k[:, :t1]) * scale
        t_idx = t0 + jnp.arange(QBLK)
        s_idx = jnp.arange(t1)
        strict = s_idx[None, :] < t_idx[:, None]
        log_keep = jnp.where(strict, jax.nn.log_sigmoid(-z), 0.0)
        suffix = lax.cumsum(log_keep, axis=3, reverse=True) - log_keep
        A = jnp.where(strict, jnp.exp(jax.nn.log_sigmoid(z) + suffix), 0.0)
        outs.append(jnp.einsum('bhqs,bshd->bqhd', A, v[:, :t1]))
    return jnp.concatenate(outs, axis=1)


def setup_inputs(seed: int = 0) -> dict:
    key = jax.random.key(seed)
    ks = jax.random.split(key, 15)
    n = jax.random.normal
    f32 = jnp.float32
    x = n(ks[0], (BATCH, SEQ, D_MODEL), f32)
    norm_mix_g = 1.0 + 0.1 * n(ks[1], (DEPTH, D_MODEL), f32)
    w_in = n(ks[2], (DEPTH, D_MODEL, IN_COLS), f32) * D_MODEL ** -0.5
    b_if = jnp.concatenate([0.5 * n(ks[3], (DEPTH, H_A), f32),
                            3.0 + 0.5 * n(ks[4], (DEPTH, H_A), f32)], axis=-1)
    b_gate = 0.02 * n(ks[5], (DEPTH, 2 * D_MODEL), f32)
    conv_w = n(ks[6], (DEPTH, CONV_K, 2 * QK_A), f32) * CONV_K ** -0.5
    mlstm_norm_g = 1.0 + 0.1 * n(ks[7], (DEPTH, V_A), f32)
    sb_q_norm_g = 1.0 + 0.1 * n(ks[8], (DEPTH, HD_B), f32)
    sb_k_norm_g = 1.0 + 0.1 * n(ks[9], (DEPTH, HD_B), f32)
    w_out = n(ks[10], (DEPTH, D_MODEL, D_MODEL), f32) * D_MODEL ** -0.5
    norm_mlp_g = 1.0 + 0.1 * n(ks[11], (DEPTH, D_MODEL), f32)
    w_up = n(ks[12], (DEPTH, D_MODEL, D_FF), f32) * D_MODEL ** -0.5
    w_down = n(ks[13], (DEPTH, D_FF, D_MODEL), f32) * D_FF ** -0.5
    return {"x": x, "norm_mix_g": norm_mix_g, "w_in": w_in, "b_if": b_if, "b_gate": b_gate,
            "conv_w": conv_w, "mlstm_norm_g": mlstm_norm_g, "sb_q_norm_g": sb_q_norm_g,
            "sb_k_norm_g": sb_k_norm_g, "w_out": w_out, "norm_mlp_g": norm_mlp_g,
            "w_up": w_up, "w_down": w_down}


def reference(x, norm_mix_g, w_in, b_if, b_gate, conv_w, mlstm_norm_g, sb_q_norm_g, sb_k_norm_g,
              w_out, norm_mlp_g, w_up, w_down):
    B, S, _ = x.shape
    split_idx = np.cumsum(IN_SIZES)[:-1].tolist()
    for l in range(DEPTH):
        h = rms_norm(x, norm_mix_g[l])
        p = h @ w_in[l]
        q_a, k_a, v_a, o_a, i_a, f_a, q_b, k_b, v_b, g_a, g_b = jnp.split(p, split_idx, axis=-1)
        qk_a = jax.nn.silu(causal_conv(jnp.concatenate([q_a, k_a], axis=-1), conv_w[l]))
        q_a, k_a = qk_a[..., :QK_A], qk_a[..., QK_A:]
        i_pre = i_a.astype(jnp.float32) + b_if[l, :H_A].astype(jnp.float32)
        f_pre = f_a.astype(jnp.float32) + b_if[l, H_A:].astype(jnp.float32)
        h_a = mlstm_chunkwise(q_a.reshape(B, S, H_A, DQK_A), k_a.reshape(B, S, H_A, DQK_A),
                              v_a.reshape(B, S, H_A, DV_A), i_pre, f_pre)
        h_a = head_rms_norm(h_a, mlstm_norm_g[l].reshape(H_A, DV_A)).reshape(B, S, V_A)
        y_a = jax.nn.sigmoid(o_a.astype(jnp.float32)) * h_a
        qn = head_rms_norm(q_b.reshape(B, S, H_B, HD_B), sb_q_norm_g[l])
        kn = head_rms_norm(k_b.reshape(B, S, H_B, HD_B), sb_k_norm_g[l])
        y_b = stick_breaking(qn, kn, v_b.reshape(B, S, H_B, HD_B).astype(jnp.float32)).reshape(B, S, W_B)
        gates = jax.nn.sigmoid(jnp.concatenate([g_a, g_b], axis=-1).astype(jnp.float32)
                               + b_gate[l].astype(jnp.float32))
        y = gates[..., :D_MODEL] * y_a + gates[..., D_MODEL:] * y_b
        x = x + y.astype(x.dtype) @ w_out[l]
        h2 = rms_norm(x, norm_mlp_g[l])
        x = x + jnp.square(jax.nn.relu(h2 @ w_up[l])) @ w_down[l]
    return x
```

```python
import functools

import jax
import jax.numpy as jnp
from jax import lax
from jax.experimental import pallas as pl
from jax.experimental.pallas import tpu as pltpu

F32 = jnp.float32
BF16 = jnp.bfloat16

EPS = 1e-6
N_HEADS = 8
HEAD_DIM = 128
DQK_A = 64
CONV_K = 4
LANES = 128
CONV_HALO = 8
VMEM_LIMIT = 48 * 1024 * 1024

BLK_QK_A, BLK_V_A, BLK_O_A, BLK_Q_B, BLK_K_B, BLK_V_B, BLK_G_A, BLK_G_B = range(8)
N_BLK = 8


def _log_sigmoid(x):
    return jnp.minimum(x, 0.0) - jnp.log1p(jnp.exp(-jnp.abs(x)))


def _sigmoid(x):
    return 1.0 / (1.0 + jnp.exp(-x))


def _inproj_kernel(x_ref, g_ref, w_ref, wg_ref, wgt_ref, qkg_ref,
                   p_ref, gates_ref, gates_t_ref, h_ref):
    j = pl.program_id(1)

    @pl.when(j == 0)
    def _():
        x = x_ref[...]
        ms = jnp.mean(x * x, axis=-1, keepdims=True)
        h = (x * lax.rsqrt(ms + EPS) * g_ref[...]).astype(BF16)
        h_ref[...] = h
        gates_ref[...] = jnp.dot(h, wg_ref[...], preferred_element_type=F32)
        gates_t_ref[...] = lax.dot_general(wgt_ref[...], h, (((1,), (1,)), ((), ())),
                                           preferred_element_type=F32)

    is_norm = jnp.logical_or(j == BLK_Q_B, j == BLK_K_B)

    @pl.when(jnp.logical_not(is_norm))
    def _():
        acc = jnp.dot(h_ref[...], w_ref[...], preferred_element_type=F32)
        p_ref[...] = acc.astype(p_ref.dtype)

    @pl.when(is_norm)
    def _():
        acc = jnp.dot(h_ref[...], w_ref[...], preferred_element_type=F32)
        gain = qkg_ref[0]
        for h in range(N_HEADS):
            sl = slice(h * HEAD_DIM, (h + 1) * HEAD_DIM)
            a = acc[:, sl]
            ms = jnp.mean(a * a, axis=-1, keepdims=True)
            p_ref[:, sl] = (a * lax.rsqrt(ms + EPS) * gain[:, sl]).astype(p_ref.dtype)


def _inproj(x2d, g, w_main, w_gate, w_gate_t, qk_gain, tm):
    T, D = x2d.shape
    n_cols = w_main.shape[1]
    grid = (T // tm, n_cols // D)
    return pl.pallas_call(
        _inproj_kernel,
        grid=grid,
        in_specs=[
            pl.BlockSpec((tm, D), lambda i, j: (i, 0)),
            pl.BlockSpec((1, D), lambda i, j: (0, 0)),
            pl.BlockSpec((D, D), lambda i, j: (0, j)),
            pl.BlockSpec((D, LANES), lambda i, j: (0, 0)),
            pl.BlockSpec((2 * N_HEADS, D), lambda i, j: (0, 0)),
            pl.BlockSpec((1, 1, D), lambda i, j: (jnp.where(j == BLK_K_B, 1, 0), 0, 0)),
        ],
        out_specs=[
            pl.BlockSpec((tm, D), lambda i, j: (i, j)),
            pl.BlockSpec((tm, LANES), lambda i, j: (i, 0)),
            pl.BlockSpec((2 * N_HEADS, tm), lambda i, j: (0, i)),
        ],
        out_shape=[
            jax.ShapeDtypeStruct((T, n_cols), BF16),
            jax.ShapeDtypeStruct((T, LANES), F32),
            jax.ShapeDtypeStruct((2 * N_HEADS, T), F32),
        ],
        scratch_shapes=[pltpu.VMEM((tm, D), BF16)],
        compiler_params=pltpu.CompilerParams(
            dimension_semantics=("parallel", "arbitrary"), vmem_limit_bytes=VMEM_LIMIT),
        name="norm_inproj",
    )(x2d, g, w_main, w_gate, w_gate_t, qk_gain)


def _mlstm_kernel(qk_ref, v_ref, o_ref, ga_ref, gates_ref, gates_t_ref,
                  convw_ref, bif_row_ref, bif_col_ref, ng_ref, bg_ref,
                  out_ref, xbuf_ref, c_ref, n_ref, m_ref, *, L):
    c = pl.program_id(1)
    d_qk = N_HEADS * DQK_A

    @pl.when(c == 0)
    def _():
        xbuf_ref[0:CONV_HALO, :] = jnp.zeros((CONV_HALO, xbuf_ref.shape[1]), F32)
        c_ref[...] = jnp.zeros(c_ref.shape, F32)
        n_ref[...] = jnp.zeros(n_ref.shape, F32)
        m_ref[...] = jnp.zeros(m_ref.shape, F32)

    xbuf_ref[CONV_HALO:CONV_HALO + L, :] = qk_ref[...].astype(F32)
    w = convw_ref[...]
    y = w[CONV_K - 1:CONV_K, :] * xbuf_ref[CONV_HALO:CONV_HALO + L, :]
    for tap in range(CONV_K - 1):
        off = CONV_HALO - (CONV_K - 1) + tap
        y = y + w[tap:tap + 1, :] * xbuf_ref[off:off + L, :]
    xbuf_ref[0:CONV_HALO, :] = xbuf_ref[L:L + CONV_HALO, :]
    qk = y * _sigmoid(y)

    row = lax.broadcasted_iota(jnp.int32, (L, L), 0)
    col = lax.broadcasted_iota(jnp.int32, (L, L), 1)
    causal = col <= row
    tri = jnp.where(causal, 1.0, 0.0).astype(F32)
    pre_col = gates_ref[...] + bif_row_ref[...]
    b_col_all = jnp.dot(tri, _log_sigmoid(pre_col), preferred_element_type=F32,
                        precision=lax.Precision.HIGHEST)
    pre_row = gates_t_ref[0] + bif_col_ref[...]
    b_row_all = lax.dot_general(_log_sigmoid(pre_row), tri, (((1,), (1,)), ((), ())),
                                preferred_element_type=F32, precision=lax.Precision.HIGHEST)

    lane = lax.broadcasted_iota(jnp.int32, (L, LANES), 1)
    for h in range(N_HEADS):
        pair, half = divmod(h, 2)
        q_pair = qk[:, pair * LANES:(pair + 1) * LANES].astype(BF16)
        k_pair = qk[:, d_qk + pair * LANES:d_qk + (pair + 1) * LANES]
        in_head = jnp.logical_and(lane >= half * DQK_A, lane < (half + 1) * DQK_A)
        km = jnp.where(in_head, k_pair * (DQK_A ** -0.5), 0.0)
        km_b = km.astype(BF16)
        vh = v_ref[:, h * HEAD_DIM:(h + 1) * HEAD_DIM]

        b_t = b_col_all[:, N_HEADS + h:N_HEADS + h + 1]
        i_t = pre_col[:, h:h + 1]
        b_s = b_row_all[N_HEADS + h:N_HEADS + h + 1, :]
        i_s = pre_row[h:h + 1, :]
        b_last = b_t[L - 1:L, :]
        m0 = m_ref[h][:, 0:1]
        c0 = c_ref[h]
        n0 = n_ref[h]

        dmat = jnp.where(causal, b_t - b_s + i_s, -jnp.inf)
        b_inter = b_t + m0
        m_t = jnp.maximum(b_inter, jnp.max(dmat, axis=-1, keepdims=True))
        s_qk = lax.dot_general(q_pair, km_b, (((1,), (1,)), ((), ())), preferred_element_type=F32)
        p = jnp.exp(dmat - m_t) * s_qk
        inter = jnp.exp(b_inter - m_t)
        num = (jnp.dot(p.astype(BF16), vh, preferred_element_type=F32)
               + inter * jnp.dot(q_pair, c0.astype(BF16), preferred_element_type=F32))
        den = (jnp.sum(p, axis=-1, keepdims=True)
               + inter * jnp.sum(q_pair.astype(F32) * n0, axis=-1, keepdims=True))
        hh = num / jnp.maximum(jnp.abs(den), jnp.exp(-m_t))
        sl = slice(h * HEAD_DIM, (h + 1) * HEAD_DIM)
        hn = hh * lax.rsqrt(jnp.mean(hh * hh, axis=-1, keepdims=True) + EPS) * ng_ref[:, sl]
        gate = _sigmoid(o_ref[:, sl].astype(F32)) * _sigmoid(ga_ref[:, sl].astype(F32) + bg_ref[:, sl])
        out_ref[:, sl] = (gate * hn).astype(out_ref.dtype)

        a_t = b_last - b_t + i_t
        m_loc = jnp.max(a_t, axis=0, keepdims=True)
        w_t = jnp.exp(a_t - m_loc)
        wv = (w_t * vh.astype(F32)).astype(BF16)
        c_loc = lax.dot_general(km_b, wv, (((0,), (0,)), ((), ())), preferred_element_type=F32)
        n_loc = jnp.sum(w_t * km, axis=0, keepdims=True)
        m_new = jnp.maximum(b_last + m0, m_loc)
        sp = jnp.exp(b_last + m0 - m_new)
        sc = jnp.exp(m_loc - m_new)
        c_ref[h] = sp * c0 + sc * c_loc
        n_ref[h] = sp * n0 + sc * n_loc
        m_ref[h] = jnp.broadcast_to(m_new, (1, LANES))


def _mlstm(p, gates, gates_t, conv_w, bif_row, bif_col, norm_g, b_gate_a, B, S, L):
    T = B * S
    D = N_HEADS * HEAD_DIM
    nc = S // L
    tok = lambda blk: pl.BlockSpec((L, D), lambda b, c: (b * nc + c, blk))
    const = lambda shape: pl.BlockSpec(shape, lambda b, c: (0,) * len(shape))
    return pl.pallas_call(
        functools.partial(_mlstm_kernel, L=L),
        grid=(B, nc),
        in_specs=[
            tok(BLK_QK_A), tok(BLK_V_A), tok(BLK_O_A), tok(BLK_G_A),
            pl.BlockSpec((L, LANES), lambda b, c: (b * nc + c, 0)),
            pl.BlockSpec((1, 2 * N_HEADS, L), lambda b, c: (b * nc + c, 0, 0)),
            const((CONV_K, D)), const((1, LANES)), const((2 * N_HEADS, 1)), const((1, D)), const((1, D)),
        ],
        out_specs=pl.BlockSpec((L, D), lambda b, c: (b * nc + c, 0)),
        out_shape=jax.ShapeDtypeStruct((T, D), BF16),
        scratch_shapes=[
            pltpu.VMEM((L + CONV_HALO, D), F32),
            pltpu.VMEM((N_HEADS, LANES, HEAD_DIM), F32),
            pltpu.VMEM((N_HEADS, 1, LANES), F32),
            pltpu.VMEM((N_HEADS, 1, LANES), F32),
        ],
        compiler_params=pltpu.CompilerParams(
            dimension_semantics=("parallel", "arbitrary"), vmem_limit_bytes=VMEM_LIMIT),
        name="mlstm",
    )(p, p, p, p, gates, gates_t, conv_w, bif_row, bif_col, norm_g, b_gate_a)


def _sb_kernel(q_ref, k_ref, v_ref, gb_ref, ya_ref, bg_ref, csm_ref, out_ref, acc_ref, carry_ref, *, tq):
    i = pl.program_id(2)
    scale = HEAD_DIM ** -0.5
    q = q_ref[...]
    csm = csm_ref[...]

    def block(j, masked):
        start = pl.multiple_of(j * tq, tq)
        kj = k_ref[pl.ds(start, tq), :]
        vj = v_ref[pl.ds(start, tq), :]
        z = lax.dot_general(q, kj, (((1,), (1,)), ((), ())), preferred_element_type=F32) * scale
        lk = -(jnp.maximum(z, 0.0) + jnp.log1p(jnp.exp(-jnp.abs(z))))
        if masked:
            row = lax.broadcasted_iota(jnp.int32, (tq, tq), 0)
            col = lax.broadcasted_iota(jnp.int32, (tq, tq), 1)
            strict = col < row
            lk = jnp.where(strict, lk, 0.0)
        cs = jnp.dot(lk.astype(BF16), csm, preferred_element_type=F32)
        carry = carry_ref[...]
        t = z + lk + cs[:, :tq] + jnp.concatenate([carry] * (tq // LANES), axis=1)
        a = jnp.exp(t)
        if masked:
            a = jnp.where(strict, a, 0.0)
        acc_ref[...] += jnp.dot(a.astype(BF16), vj, preferred_element_type=F32)
        carry_ref[...] = carry + cs[:, tq:]

    acc_ref[...] = jnp.zeros(acc_ref.shape, F32)
    carry_ref[...] = jnp.zeros(carry_ref.shape, F32)
    block(i, True)

    def body(it, _):
        block(i - 1 - it, False)
        return 0

    lax.fori_loop(0, i, body, 0)

    gate = _sigmoid(gb_ref[...].astype(F32) + bg_ref[0])
    out_ref[...] = (ya_ref[...].astype(F32) + gate * acc_ref[...]).astype(out_ref.dtype)


def _stick_breaking(p, ya, b_gate_b, csm, B, S, tq):
    T = B * S
    D = N_HEADS * HEAD_DIM
    nq = S // tq
    tile = lambda blk: pl.BlockSpec((tq, HEAD_DIM), lambda b, h, i: (b * nq + i, blk * N_HEADS + h))
    seq = lambda blk: pl.BlockSpec((S, HEAD_DIM), lambda b, h, i: (b, blk * N_HEADS + h))
    return pl.pallas_call(
        functools.partial(_sb_kernel, tq=tq),
        grid=(B, N_HEADS, nq),
        in_specs=[
            tile(BLK_Q_B), seq(BLK_K_B), seq(BLK_V_B), tile(BLK_G_B),
            pl.BlockSpec((tq, HEAD_DIM), lambda b, h, i: (b * nq + i, h)),
            pl.BlockSpec((1, 1, HEAD_DIM), lambda b, h, i: (h, 0, 0)),
            pl.BlockSpec((tq, tq + LANES), lambda b, h, i: (0, 0)),
        ],
        out_specs=pl.BlockSpec((tq, HEAD_DIM), lambda b, h, i: (b * nq + i, h)),
        out_shape=jax.ShapeDtypeStruct((T, D), BF16),
        scratch_shapes=[pltpu.VMEM((tq, HEAD_DIM), F32), pltpu.VMEM((tq, LANES), F32)],
        compiler_params=pltpu.CompilerParams(
            dimension_semantics=("parallel", "parallel", "arbitrary"), vmem_limit_bytes=VMEM_LIMIT),
        name="stick_breaking",
    )(p, p, p, p, ya, b_gate_b, csm)


def _mlp_kernel(x_ref, y_ref, wo_ref, g_ref, wu_ref, wd_ref, out_ref, x1_ref, h_ref, acc_ref):
    f = pl.program_id(1)

    @pl.when(f == 0)
    def _():
        x1 = x_ref[...] + jnp.dot(y_ref[...], wo_ref[...], preferred_element_type=F32)
        x1_ref[...] = x1
        ms = jnp.mean(x1 * x1, axis=-1, keepdims=True)
        h_ref[...] = (x1 * lax.rsqrt(ms + EPS) * g_ref[...]).astype(BF16)
        acc_ref[...] = jnp.zeros(acc_ref.shape, F32)

    u = jnp.maximum(jnp.dot(h_ref[...], wu_ref[...], preferred_element_type=F32), 0.0)
    acc_ref[...] += jnp.dot((u * u).astype(BF16), wd_ref[...], preferred_element_type=F32)

    @pl.when(f == pl.num_programs(1) - 1)
    def _():
        out_ref[...] = x1_ref[...] + acc_ref[...]


def _outproj_mlp(x2d, y, w_out, g, w_up, w_down, tm, tf):
    T, D = x2d.shape
    F = w_up.shape[1]
    return pl.pallas_call(
        _mlp_kernel,
        grid=(T // tm, F // tf),
        in_specs=[
            pl.BlockSpec((tm, D), lambda i, f: (i, 0)),
            pl.BlockSpec((tm, D), lambda i, f: (i, 0)),
            pl.BlockSpec((D, D), lambda i, f: (0, 0)),
            pl.BlockSpec((1, D), lambda i, f: (0, 0)),
            pl.BlockSpec((D, tf), lambda i, f: (0, f)),
            pl.BlockSpec((tf, D), lambda i, f: (f, 0)),
        ],
        out_specs=pl.BlockSpec((tm, D), lambda i, f: (i, 0)),
        out_shape=jax.ShapeDtypeStruct((T, D), F32),
        scratch_shapes=[pltpu.VMEM((tm, D), F32), pltpu.VMEM((tm, D), BF16), pltpu.VMEM((tm, D), F32)],
        compiler_params=pltpu.CompilerParams(
            dimension_semantics=("parallel", "arbitrary"), vmem_limit_bytes=VMEM_LIMIT),
        name="outproj_mlp",
    )(x2d, y, w_out, g, w_up, w_down)


def _tiles(B, S):
    T = B * S
    return dict(
        tm_in=min(1024, T),
        chunk=min(64, S),
        tq=min(256, S),
        tm_mlp=min(512, T),
        tf=512,
    )


def kernel(x, norm_mix_g, w_in, b_if, b_gate, conv_w, mlstm_norm_g, sb_q_norm_g, sb_k_norm_g,
           w_out, norm_mlp_g, w_up, w_down):
    B, S, D = x.shape
    depth = w_in.shape[0]
    T = B * S
    assert D == N_HEADS * HEAD_DIM and w_in.shape[2] == 8 * D + 2 * N_HEADS
    cfg = _tiles(B, S)
    tq = cfg["tq"]
    gate_lo = 3 * D
    gate_hi = gate_lo + 2 * N_HEADS

    r = lax.broadcasted_iota(jnp.int32, (tq, tq), 0)
    c = lax.broadcasted_iota(jnp.int32, (tq, tq), 1)
    csm = jnp.concatenate([jnp.where(r > c, 1.0, 0.0), jnp.ones((tq, LANES), F32)], axis=1).astype(BF16)

    x2d = x.reshape(T, D)
    for l in range(depth):
        w_main = jnp.concatenate([w_in[l][:, :gate_lo], w_in[l][:, gate_hi:]], axis=1).astype(BF16)
        w_gate = jnp.pad(w_in[l][:, gate_lo:gate_hi], ((0, 0), (0, LANES - 2 * N_HEADS))).astype(BF16)
        w_gate_t = w_in[l][:, gate_lo:gate_hi].T.astype(BF16)
        qk_gain = jnp.stack([jnp.tile(sb_q_norm_g[l], N_HEADS), jnp.tile(sb_k_norm_g[l], N_HEADS)]
                            ).reshape(2, 1, D)
        p, gates, gates_t = _inproj(x2d, norm_mix_g[l].reshape(1, D), w_main, w_gate, w_gate_t,
                                    qk_gain, cfg["tm_in"])
        bif_row = jnp.pad(b_if[l], (0, LANES - 2 * N_HEADS)).reshape(1, LANES)
        bif_col = b_if[l].reshape(2 * N_HEADS, 1)
        L = cfg["chunk"]
        gates_t = gates_t.reshape(2 * N_HEADS, T // L, L).transpose(1, 0, 2)
        ya = _mlstm(p, gates, gates_t, conv_w[l], bif_row, bif_col, mlstm_norm_g[l].reshape(1, D),
                    b_gate[l][:D].reshape(1, D), B, S, cfg["chunk"])
        y = _stick_breaking(p, ya, b_gate[l][D:].reshape(N_HEADS, 1, HEAD_DIM), csm, B, S, tq)
        x2d = _outproj_mlp(x2d, y, w_out[l].astype(BF16), norm_mlp_g[l].reshape(1, D),
                           w_up[l].astype(BF16), w_down[l].astype(BF16), cfg["tm_mlp"], cfg["tf"])
    return x2d.reshape(B, S, D)
```

```python
import functools

import jax
import jax.numpy as jnp
from jax import lax
from jax.experimental import pallas as pl
from jax.experimental.pallas import tpu as pltpu

F32 = jnp.float32
BF16 = jnp.bfloat16

EPS = 1e-6
N_HEADS = 8
HEAD_DIM = 128
DQK_A = 64
CONV_K = 4
LANES = 128
CONV_HALO = 8
VMEM_LIMIT = 56 * 1024 * 1024
UNDERFLOW_LOG = -110.0

BLK_QK_A, BLK_V_A, BLK_O_A, BLK_Q_B, BLK_K_B, BLK_V_B, BLK_G_A, BLK_G_B = range(8)
N_BLK = 8


def _log_sigmoid(x):
    return jnp.minimum(x, 0.0) - jnp.log1p(jnp.exp(-jnp.abs(x)))


def _sigmoid(x):
    return 1.0 / (1.0 + jnp.exp(-x))


def _inproj_kernel(x_ref, g_ref, w_ref, wg_ref, wgt_ref, qkg_ref,
                   p_ref, gates_ref, gates_t_ref, h_ref):
    j = pl.program_id(1)

    @pl.when(j == 0)
    def _():
        x = x_ref[...]
        ms = jnp.mean(x * x, axis=-1, keepdims=True)
        h = (x * lax.rsqrt(ms + EPS) * g_ref[...]).astype(BF16)
        h_ref[...] = h
        gates_ref[...] = jnp.dot(h, wg_ref[...], preferred_element_type=F32)
        gates_t_ref[...] = lax.dot_general(wgt_ref[...], h, (((1,), (1,)), ((), ())),
                                           preferred_element_type=F32)

    is_norm = jnp.logical_or(j == BLK_Q_B, j == BLK_K_B)

    @pl.when(jnp.logical_not(is_norm))
    def _():
        acc = jnp.dot(h_ref[...], w_ref[...], preferred_element_type=F32)
        p_ref[...] = acc.astype(p_ref.dtype)

    @pl.when(is_norm)
    def _():
        acc = jnp.dot(h_ref[...], w_ref[...], preferred_element_type=F32)
        gain = qkg_ref[0]
        for h in range(N_HEADS):
            sl = slice(h * HEAD_DIM, (h + 1) * HEAD_DIM)
            a = acc[:, sl]
            ms = jnp.mean(a * a, axis=-1, keepdims=True)
            p_ref[:, sl] = (a * lax.rsqrt(ms + EPS) * gain[:, sl]).astype(p_ref.dtype)


def _inproj(x2d, g, w_main, w_gate, w_gate_t, qk_gain, tm):
    T, D = x2d.shape
    n_cols = w_main.shape[1]
    grid = (T // tm, n_cols // D)
    return pl.pallas_call(
        _inproj_kernel,
        grid=grid,
        in_specs=[
            pl.BlockSpec((tm, D), lambda i, j: (i, 0)),
            pl.BlockSpec((1, D), lambda i, j: (0, 0)),
            pl.BlockSpec((D, D), lambda i, j: (0, j)),
            pl.BlockSpec((D, LANES), lambda i, j: (0, 0)),
            pl.BlockSpec((2 * N_HEADS, D), lambda i, j: (0, 0)),
            pl.BlockSpec((1, 1, D), lambda i, j: (jnp.where(j == BLK_K_B, 1, 0), 0, 0)),
        ],
        out_specs=[
            pl.BlockSpec((tm, D), lambda i, j: (i, j)),
            pl.BlockSpec((tm, LANES), lambda i, j: (i, 0)),
            pl.BlockSpec((2 * N_HEADS, tm), lambda i, j: (0, i)),
        ],
        out_shape=[
            jax.ShapeDtypeStruct((T, n_cols), BF16),
            jax.ShapeDtypeStruct((T, LANES), F32),
            jax.ShapeDtypeStruct((2 * N_HEADS, T), F32),
        ],
        scratch_shapes=[pltpu.VMEM((tm, D), BF16)],
        compiler_params=pltpu.CompilerParams(
            dimension_semantics=("parallel", "arbitrary"), vmem_limit_bytes=VMEM_LIMIT),
        name="norm_inproj",
    )(x2d, g, w_main, w_gate, w_gate_t, qk_gain)


def _mlstm_kernel(qk_ref, v_ref, o_ref, ga_ref, gates_ref, gates_t_ref,
                  convw_ref, bif_row_ref, bif_col_ref, ng_ref, bg_ref,
                  out_ref, xbuf_ref, c_ref, n_ref, m_ref, *, L):
    c = pl.program_id(1)
    d_qk = N_HEADS * DQK_A

    @pl.when(c == 0)
    def _():
        xbuf_ref[0:CONV_HALO, :] = jnp.zeros((CONV_HALO, xbuf_ref.shape[1]), F32)
        c_ref[...] = jnp.zeros(c_ref.shape, F32)
        n_ref[...] = jnp.zeros(n_ref.shape, F32)
        m_ref[...] = jnp.zeros(m_ref.shape, F32)

    xbuf_ref[CONV_HALO:CONV_HALO + L, :] = qk_ref[...].astype(F32)
    w = convw_ref[...]
    y = w[CONV_K - 1:CONV_K, :] * xbuf_ref[CONV_HALO:CONV_HALO + L, :]
    for tap in range(CONV_K - 1):
        off = CONV_HALO - (CONV_K - 1) + tap
        y = y + w[tap:tap + 1, :] * xbuf_ref[off:off + L, :]
    xbuf_ref[0:CONV_HALO, :] = xbuf_ref[L:L + CONV_HALO, :]
    qk = y * _sigmoid(y)

    row = lax.broadcasted_iota(jnp.int32, (L, L), 0)
    col = lax.broadcasted_iota(jnp.int32, (L, L), 1)
    causal = col <= row
    tri = jnp.where(causal, 1.0, 0.0).astype(F32)
    pre_col = gates_ref[...] + bif_row_ref[...]
    b_col_all = jnp.dot(tri, _log_sigmoid(pre_col), preferred_element_type=F32,
                        precision=lax.Precision.HIGHEST)
    pre_row = gates_t_ref[0] + bif_col_ref[...]
    b_row_all = lax.dot_general(_log_sigmoid(pre_row), tri, (((1,), (1,)), ((), ())),
                                preferred_element_type=F32, precision=lax.Precision.HIGHEST)

    lane = lax.broadcasted_iota(jnp.int32, (L, LANES), 1)
    for h in range(N_HEADS):
        pair, half = divmod(h, 2)
        q_pair = qk[:, pair * LANES:(pair + 1) * LANES].astype(BF16)
        k_pair = qk[:, d_qk + pair * LANES:d_qk + (pair + 1) * LANES]
        in_head = jnp.logical_and(lane >= half * DQK_A, lane < (half + 1) * DQK_A)
        km = jnp.where(in_head, k_pair * (DQK_A ** -0.5), 0.0)
        km_b = km.astype(BF16)
        vh = v_ref[:, h * HEAD_DIM:(h + 1) * HEAD_DIM]

        b_t = b_col_all[:, N_HEADS + h:N_HEADS + h + 1]
        i_t = pre_col[:, h:h + 1]
        b_s = b_row_all[N_HEADS + h:N_HEADS + h + 1, :]
        i_s = pre_row[h:h + 1, :]
        b_last = b_t[L - 1:L, :]
        m0 = m_ref[h][:, 0:1]
        c0 = c_ref[h]
        n0 = n_ref[h]

        dmat = jnp.where(causal, b_t - b_s + i_s, -jnp.inf)
        b_inter = b_t + m0
        m_t = jnp.maximum(b_inter, jnp.max(dmat, axis=-1, keepdims=True))
        s_qk = lax.dot_general(q_pair, km_b, (((1,), (1,)), ((), ())), preferred_element_type=F32)
        p = jnp.exp(dmat - m_t) * s_qk
        inter = jnp.exp(b_inter - m_t)
        num = (jnp.dot(p.astype(BF16), vh, preferred_element_type=F32)
               + inter * jnp.dot(q_pair, c0.astype(BF16), preferred_element_type=F32))
        den = (jnp.sum(p, axis=-1, keepdims=True)
               + inter * jnp.sum(q_pair.astype(F32) * n0, axis=-1, keepdims=True))
        hh = num / jnp.maximum(jnp.abs(den), jnp.exp(-m_t))
        sl = slice(h * HEAD_DIM, (h + 1) * HEAD_DIM)
        hn = hh * lax.rsqrt(jnp.mean(hh * hh, axis=-1, keepdims=True) + EPS) * ng_ref[:, sl]
        gate = _sigmoid(o_ref[:, sl].astype(F32)) * _sigmoid(ga_ref[:, sl].astype(F32) + bg_ref[:, sl])
        out_ref[:, sl] = (gate * hn).astype(out_ref.dtype)

        a_t = b_last - b_t + i_t
        m_loc = jnp.max(a_t, axis=0, keepdims=True)
        w_t = jnp.exp(a_t - m_loc)
        wv = (w_t * vh.astype(F32)).astype(BF16)
        c_loc = lax.dot_general(km_b, wv, (((0,), (0,)), ((), ())), preferred_element_type=F32)
        n_loc = jnp.sum(w_t * km, axis=0, keepdims=True)
        m_new = jnp.maximum(b_last + m0, m_loc)
        sp = jnp.exp(b_last + m0 - m_new)
        sc = jnp.exp(m_loc - m_new)
        c_ref[h] = sp * c0 + sc * c_loc
        n_ref[h] = sp * n0 + sc * n_loc
        m_ref[h] = jnp.broadcast_to(m_new, (1, LANES))


def _mlstm(p, gates, gates_t, conv_w, bif_row, bif_col, norm_g, b_gate_a, B, S, L):
    T = B * S
    D = N_HEADS * HEAD_DIM
    nc = S // L
    tok = lambda blk: pl.BlockSpec((L, D), lambda b, c: (b * nc + c, blk))
    const = lambda shape: pl.BlockSpec(shape, lambda b, c: (0,) * len(shape))
    return pl.pallas_call(
        functools.partial(_mlstm_kernel, L=L),
        grid=(B, nc),
        in_specs=[
            tok(BLK_QK_A), tok(BLK_V_A), tok(BLK_O_A), tok(BLK_G_A),
            pl.BlockSpec((L, LANES), lambda b, c: (b * nc + c, 0)),
            pl.BlockSpec((1, 2 * N_HEADS, L), lambda b, c: (b * nc + c, 0, 0)),
            const((CONV_K, D)), const((1, LANES)), const((2 * N_HEADS, 1)), const((1, D)), const((1, D)),
        ],
        out_specs=pl.BlockSpec((L, D), lambda b, c: (b * nc + c, 0)),
        out_shape=jax.ShapeDtypeStruct((T, D), BF16),
        scratch_shapes=[
            pltpu.VMEM((L + CONV_HALO, D), F32),
            pltpu.VMEM((N_HEADS, LANES, HEAD_DIM), F32),
            pltpu.VMEM((N_HEADS, 1, LANES), F32),
            pltpu.VMEM((N_HEADS, 1, LANES), F32),
        ],
        compiler_params=pltpu.CompilerParams(
            dimension_semantics=("parallel", "arbitrary"), vmem_limit_bytes=VMEM_LIMIT),
        name="mlstm",
    )(p, p, p, p, gates, gates_t, conv_w, bif_row, bif_col, norm_g, b_gate_a)


def _sb_kernel(q_ref, k_ref, v_ref, gb_ref, ya_ref, bg_ref, csm_ref, out_ref, acc_ref, carry_ref, *, tq):
    i = pl.program_id(2)
    q = q_ref[...]
    csm = csm_ref[...]
    reps = tq // LANES

    def keys(j, n):
        return k_ref[pl.ds(pl.multiple_of(j * tq, tq), n * tq), :]

    def values(j, n):
        return v_ref[pl.ds(pl.multiple_of(j * tq, tq), n * tq), :]

    def scores(kj):
        z = lax.dot_general(q, kj, (((1,), (1,)), ((), ())), preferred_element_type=F32)
        return z, jnp.maximum(z, 0.0) + jnp.log(1.0 + jnp.exp(-jnp.abs(z)))

    def suffix(nlk):
        return jnp.dot(nlk.astype(BF16), csm, preferred_element_type=F32)

    def strict_mask():
        row = lax.broadcasted_iota(jnp.int32, (tq, tq), 0)
        col = lax.broadcasted_iota(jnp.int32, (tq, tq), 1)
        return col < row

    @pl.when(i == 0)
    def _():
        strict = strict_mask()
        z, nlk = scores(keys(0, 1))
        nlk = jnp.where(strict, nlk, 0.0)
        cs = suffix(nlk)
        a = jnp.where(strict, jnp.exp(z - nlk + cs[:, :tq]), 0.0)
        acc_ref[...] = jnp.dot(a.astype(BF16), values(0, 1), preferred_element_type=F32)

    @pl.when(i > 0)
    def _():
        strict = strict_mask()
        z, nlk = scores(keys(i - 1, 2))
        z_d, z_p = z[:, tq:], z[:, :tq]
        nlk_d, nlk_p = jnp.where(strict, nlk[:, tq:], 0.0), nlk[:, :tq]
        cs_d, cs_p = suffix(nlk_d), suffix(nlk_p)
        a_d = jnp.where(strict, jnp.exp(z_d - nlk_d + cs_d[:, :tq]), 0.0)
        carry = cs_d[:, tq:]
        a_p = jnp.exp(z_p - nlk_p + cs_p[:, :tq] + jnp.concatenate([carry] * reps, axis=1))
        a = jnp.concatenate([a_p, a_d], axis=1).astype(BF16)
        acc_ref[...] = jnp.dot(a, values(i - 1, 2), preferred_element_type=F32)
        carry = carry + cs_p[:, tq:]
        carry_ref[...] = carry

        def cond(state):
            j, alive = state
            return jnp.logical_and(j >= 0, alive > 0)

        def body(state):
            j, _ = state
            z, nlk = scores(keys(j, 1))
            cs = suffix(nlk)
            carry = carry_ref[...]
            a = jnp.exp(z - nlk + cs[:, :tq] + jnp.concatenate([carry] * reps, axis=1))
            acc_ref[...] += jnp.dot(a.astype(BF16), values(j, 1), preferred_element_type=F32)
            carry = carry + cs[:, tq:]
            carry_ref[...] = carry
            return j - 1, (jnp.max(carry) > UNDERFLOW_LOG).astype(jnp.int32)

        lax.while_loop(cond, body, (i - 2, (jnp.max(carry) > UNDERFLOW_LOG).astype(jnp.int32)))

    gate = _sigmoid(gb_ref[...].astype(F32) + bg_ref[0])
    out_ref[...] = (ya_ref[...].astype(F32) + gate * acc_ref[...]).astype(out_ref.dtype)


def _stick_breaking(p, ya, b_gate_b, csm, B, S, tq):
    T = B * S
    D = N_HEADS * HEAD_DIM
    nq = S // tq
    tile = lambda blk: pl.BlockSpec((tq, HEAD_DIM), lambda b, h, i: (b * nq + i, blk * N_HEADS + h))
    seq = lambda blk: pl.BlockSpec((S, HEAD_DIM), lambda b, h, i: (b, blk * N_HEADS + h))
    return pl.pallas_call(
        functools.partial(_sb_kernel, tq=tq),
        grid=(B, N_HEADS, nq),
        in_specs=[
            tile(BLK_Q_B), seq(BLK_K_B), seq(BLK_V_B), tile(BLK_G_B),
            pl.BlockSpec((tq, HEAD_DIM), lambda b, h, i: (b * nq + i, h)),
            pl.BlockSpec((1, 1, HEAD_DIM), lambda b, h, i: (h, 0, 0)),
            pl.BlockSpec((tq, tq + LANES), lambda b, h, i: (0, 0)),
        ],
        out_specs=pl.BlockSpec((tq, HEAD_DIM), lambda b, h, i: (b * nq + i, h)),
        out_shape=jax.ShapeDtypeStruct((T, D), BF16),
        scratch_shapes=[pltpu.VMEM((tq, HEAD_DIM), F32), pltpu.VMEM((tq, LANES), F32)],
        compiler_params=pltpu.CompilerParams(
            dimension_semantics=("parallel", "parallel", "arbitrary"), vmem_limit_bytes=VMEM_LIMIT),
        name="stick_breaking",
    )(p, p, p, p, ya, b_gate_b, csm)


def _mlp_kernel(x_ref, y_ref, wo_ref, g_ref, wu_ref, wd_ref, out_ref, x1_ref, h_ref, acc_ref):
    f = pl.program_id(1)

    @pl.when(f == 0)
    def _():
        x1 = x_ref[...] + jnp.dot(y_ref[...], wo_ref[...], preferred_element_type=F32)
        x1_ref[...] = x1
        ms = jnp.mean(x1 * x1, axis=-1, keepdims=True)
        h_ref[...] = (x1 * lax.rsqrt(ms + EPS) * g_ref[...]).astype(BF16)
        acc_ref[...] = jnp.zeros(acc_ref.shape, F32)

    u = jnp.maximum(jnp.dot(h_ref[...], wu_ref[...], preferred_element_type=F32), 0.0)
    acc_ref[...] += jnp.dot((u * u).astype(BF16), wd_ref[...], preferred_element_type=F32)

    @pl.when(f == pl.num_programs(1) - 1)
    def _():
        out_ref[...] = x1_ref[...] + acc_ref[...]


def _outproj_mlp(x2d, y, w_out, g, w_up, w_down, tm, tf):
    T, D = x2d.shape
    F = w_up.shape[1]
    return pl.pallas_call(
        _mlp_kernel,
        grid=(T // tm, F // tf),
        in_specs=[
            pl.BlockSpec((tm, D), lambda i, f: (i, 0)),
            pl.BlockSpec((tm, D), lambda i, f: (i, 0)),
            pl.BlockSpec((D, D), lambda i, f: (0, 0)),
            pl.BlockSpec((1, D), lambda i, f: (0, 0)),
            pl.BlockSpec((D, tf), lambda i, f: (0, f)),
            pl.BlockSpec((tf, D), lambda i, f: (f, 0)),
        ],
        out_specs=pl.BlockSpec((tm, D), lambda i, f: (i, 0)),
        out_shape=jax.ShapeDtypeStruct((T, D), F32),
        scratch_shapes=[pltpu.VMEM((tm, D), F32), pltpu.VMEM((tm, D), BF16), pltpu.VMEM((tm, D), F32)],
        compiler_params=pltpu.CompilerParams(
            dimension_semantics=("parallel", "arbitrary"), vmem_limit_bytes=VMEM_LIMIT),
        name="outproj_mlp",
    )(x2d, y, w_out, g, w_up, w_down)


def _tiles(B, S):
    T = B * S
    return dict(
        tm_in=min(1024, T),
        chunk=min(256, S),
        tq=min(256, S),
        tm_mlp=min(1024, T),
        tf=512,
    )


def kernel(x, norm_mix_g, w_in, b_if, b_gate, conv_w, mlstm_norm_g, sb_q_norm_g, sb_k_norm_g,
           w_out, norm_mlp_g, w_up, w_down):
    B, S, D = x.shape
    depth = w_in.shape[0]
    T = B * S
    assert D == N_HEADS * HEAD_DIM and w_in.shape[2] == 8 * D + 2 * N_HEADS
    cfg = _tiles(B, S)
    tq = cfg["tq"]
    gate_lo = 3 * D
    gate_hi = gate_lo + 2 * N_HEADS

    r = lax.broadcasted_iota(jnp.int32, (tq, tq), 0)
    c = lax.broadcasted_iota(jnp.int32, (tq, tq), 1)
    csm = -jnp.concatenate([jnp.where(r > c, 1.0, 0.0), jnp.ones((tq, LANES), F32)], axis=1).astype(BF16)

    x2d = x.reshape(T, D)
    for l in range(depth):
        w_main = jnp.concatenate([w_in[l][:, :gate_lo], w_in[l][:, gate_hi:]], axis=1).astype(BF16)
        w_gate = jnp.pad(w_in[l][:, gate_lo:gate_hi], ((0, 0), (0, LANES - 2 * N_HEADS))).astype(BF16)
        w_gate_t = w_in[l][:, gate_lo:gate_hi].T.astype(BF16)
        qk_gain = jnp.stack([jnp.tile(sb_q_norm_g[l], N_HEADS),
                             jnp.tile(sb_k_norm_g[l] * HEAD_DIM ** -0.5, N_HEADS)]
                            ).reshape(2, 1, D)
        p, gates, gates_t = _inproj(x2d, norm_mix_g[l].reshape(1, D), w_main, w_gate, w_gate_t,
                                    qk_gain, cfg["tm_in"])
        bif_row = jnp.pad(b_if[l], (0, LANES - 2 * N_HEADS)).reshape(1, LANES)
        bif_col = b_if[l].reshape(2 * N_HEADS, 1)
        L = cfg["chunk"]
        gates_t = gates_t.reshape(2 * N_HEADS, T // L, L).transpose(1, 0, 2)
        ya = _mlstm(p, gates, gates_t, conv_w[l], bif_row, bif_col, mlstm_norm_g[l].reshape(1, D),
                    b_gate[l][:D].reshape(1, D), B, S, cfg["chunk"])
        y = _stick_breaking(p, ya, b_gate[l][D:].reshape(N_HEADS, 1, HEAD_DIM), csm, B, S, tq)
        x2d = _outproj_mlp(x2d, y, w_out[l].astype(BF16), norm_mlp_g[l].reshape(1, D),
                           w_up[l].astype(BF16), w_down[l].astype(BF16), cfg["tm_mlp"], cfg["tf"])
    return x2d.reshape(B, S, D)
```

```python
import functools

import jax
import jax.numpy as jnp
from jax import lax
from jax.experimental import pallas as pl
from jax.experimental.pallas import tpu as pltpu

F32 = jnp.float32
BF16 = jnp.bfloat16

EPS = 1e-6
N_HEADS = 8
HEAD_DIM = 128
DQK_A = 64
CONV_K = 4
LANES = 128
CONV_HALO = 8
VMEM_LIMIT = 56 * 1024 * 1024
LOG2_E = 1.4426950408889634
UNDERFLOW_LOG2 = -110.0 * LOG2_E

BLK_QK_A, BLK_V_A, BLK_O_A, BLK_G_A, BLK_G_B, BLK_V_B, BLK_Q_B, BLK_K_B = range(8)
W_IN_ORDER = (BLK_QK_A, BLK_V_A, BLK_O_A, BLK_Q_B, BLK_K_B, BLK_V_B, BLK_G_A, BLK_G_B)
N_BLK = 8
INPROJ_TN = 2


def _log_sigmoid(x):
    return jnp.minimum(x, 0.0) - jnp.log1p(jnp.exp(-jnp.abs(x)))


def _sigmoid(x):
    return 1.0 / (1.0 + jnp.exp(-x))


def _inproj_kernel(x_ref, g_ref, w_ref, wg_ref, wgt_ref, qkg_ref,
                   p_ref, gates_ref, gates_t_ref, h_ref):
    j = pl.program_id(1)

    @pl.when(j == 0)
    def _():
        x = x_ref[...]
        ms = jnp.mean(x * x, axis=-1, keepdims=True)
        h = (x * lax.rsqrt(ms + EPS) * g_ref[...]).astype(BF16)
        h_ref[...] = h
        gates_ref[...] = jnp.dot(h, wg_ref[...], preferred_element_type=F32)
        gates_t_ref[...] = lax.dot_general(wgt_ref[...], h, (((1,), (1,)), ((), ())),
                                           preferred_element_type=F32)

    is_norm = j == pl.num_programs(1) - 1

    @pl.when(jnp.logical_not(is_norm))
    def _():
        acc = jnp.dot(h_ref[...], w_ref[...], preferred_element_type=F32)
        p_ref[...] = acc.astype(p_ref.dtype)

    @pl.when(is_norm)
    def _():
        acc = jnp.dot(h_ref[...], w_ref[...], preferred_element_type=F32)
        gain = qkg_ref[...]
        for h in range(2 * N_HEADS):
            sl = slice(h * HEAD_DIM, (h + 1) * HEAD_DIM)
            a = acc[:, sl]
            ms = jnp.mean(a * a, axis=-1, keepdims=True)
            p_ref[:, sl] = (a * lax.rsqrt(ms + EPS) * gain[:, sl]).astype(p_ref.dtype)


def _inproj(x2d, g, w_main, w_gate, w_gate_t, qk_gain, tm):
    T, D = x2d.shape
    n_cols = w_main.shape[1]
    tn = INPROJ_TN * D
    assert (BLK_Q_B, BLK_K_B) == (N_BLK - 2, N_BLK - 1) and qk_gain.shape == (1, tn)
    grid = (T // tm, n_cols // tn)
    return pl.pallas_call(
        _inproj_kernel,
        grid=grid,
        in_specs=[
            pl.BlockSpec((tm, D), lambda i, j: (i, 0)),
            pl.BlockSpec((1, D), lambda i, j: (0, 0)),
            pl.BlockSpec((D, tn), lambda i, j: (0, j)),
            pl.BlockSpec((D, LANES), lambda i, j: (0, 0)),
            pl.BlockSpec((2 * N_HEADS, D), lambda i, j: (0, 0)),
            pl.BlockSpec((1, tn), lambda i, j: (0, 0)),
        ],
        out_specs=[
            pl.BlockSpec((tm, tn), lambda i, j: (i, j)),
            pl.BlockSpec((tm, LANES), lambda i, j: (i, 0)),
            pl.BlockSpec((2 * N_HEADS, tm), lambda i, j: (0, i)),
        ],
        out_shape=[
            jax.ShapeDtypeStruct((T, n_cols), BF16),
            jax.ShapeDtypeStruct((T, LANES), F32),
            jax.ShapeDtypeStruct((2 * N_HEADS, T), F32),
        ],
        scratch_shapes=[pltpu.VMEM((tm, D), BF16)],
        compiler_params=pltpu.CompilerParams(
            dimension_semantics=("parallel", "arbitrary"), vmem_limit_bytes=VMEM_LIMIT),
        name="norm_inproj",
    )(x2d, g, w_main, w_gate, w_gate_t, qk_gain)


def _mlstm_kernel(qk_ref, v_ref, o_ref, ga_ref, gates_ref, gates_t_ref,
                  convw_ref, bif_row_ref, bif_col_ref, ng_ref, bg_ref,
                  out_ref, xbuf_ref, c_ref, n_ref, m_ref, *, L):
    c = pl.program_id(1)
    d_qk = N_HEADS * DQK_A

    @pl.when(c == 0)
    def _():
        xbuf_ref[0:CONV_HALO, :] = jnp.zeros((CONV_HALO, xbuf_ref.shape[1]), F32)
        c_ref[...] = jnp.zeros(c_ref.shape, F32)
        n_ref[...] = jnp.zeros(n_ref.shape, F32)
        m_ref[...] = jnp.zeros(m_ref.shape, F32)

    xbuf_ref[CONV_HALO:CONV_HALO + L, :] = qk_ref[...].astype(F32)
    w = convw_ref[...]
    y = w[CONV_K - 1:CONV_K, :] * xbuf_ref[CONV_HALO:CONV_HALO + L, :]
    for tap in range(CONV_K - 1):
        off = CONV_HALO - (CONV_K - 1) + tap
        y = y + w[tap:tap + 1, :] * xbuf_ref[off:off + L, :]
    xbuf_ref[0:CONV_HALO, :] = xbuf_ref[L:L + CONV_HALO, :]
    qk = y * _sigmoid(y)

    row = lax.broadcasted_iota(jnp.int32, (L, L), 0)
    col = lax.broadcasted_iota(jnp.int32, (L, L), 1)
    causal = col <= row
    tri = jnp.where(causal, 1.0, 0.0).astype(F32)
    pre_col = gates_ref[...] + bif_row_ref[...]
    b_col_all = jnp.dot(tri, _log_sigmoid(pre_col), preferred_element_type=F32,
                        precision=lax.Precision.HIGHEST)
    pre_row = gates_t_ref[0] + bif_col_ref[...]
    b_row_all = lax.dot_general(_log_sigmoid(pre_row), tri, (((1,), (1,)), ((), ())),
                                preferred_element_type=F32, precision=lax.Precision.HIGHEST)

    lane = lax.broadcasted_iota(jnp.int32, (L, LANES), 1)
    for h in range(N_HEADS):
        pair, half = divmod(h, 2)
        q_pair = qk[:, pair * LANES:(pair + 1) * LANES].astype(BF16)
        k_pair = qk[:, d_qk + pair * LANES:d_qk + (pair + 1) * LANES]
        in_head = jnp.logical_and(lane >= half * DQK_A, lane < (half + 1) * DQK_A)
        km = jnp.where(in_head, k_pair * (DQK_A ** -0.5), 0.0)
        km_b = km.astype(BF16)
        vh = v_ref[:, h * HEAD_DIM:(h + 1) * HEAD_DIM]

        b_t = b_col_all[:, N_HEADS + h:N_HEADS + h + 1]
        i_t = pre_col[:, h:h + 1]
        b_s = b_row_all[N_HEADS + h:N_HEADS + h + 1, :]
        i_s = pre_row[h:h + 1, :]
        b_last = b_t[L - 1:L, :]
        m0 = m_ref[h][:, 0:1]
        c0 = c_ref[h]
        n0 = n_ref[h]

        dmat = jnp.where(causal, b_t - b_s + i_s, -jnp.inf)
        b_inter = b_t + m0
        m_t = jnp.maximum(b_inter, jnp.max(dmat, axis=-1, keepdims=True))
        s_qk = lax.dot_general(q_pair, km_b, (((1,), (1,)), ((), ())), preferred_element_type=F32)
        p = jnp.exp(dmat - m_t) * s_qk
        inter = jnp.exp(b_inter - m_t)
        num = (jnp.dot(p.astype(BF16), vh, preferred_element_type=F32)
               + inter * jnp.dot(q_pair, c0.astype(BF16), preferred_element_type=F32))
        den = (jnp.sum(p, axis=-1, keepdims=True)
               + inter * jnp.sum(q_pair.astype(F32) * n0, axis=-1, keepdims=True))
        hh = num / jnp.maximum(jnp.abs(den), jnp.exp(-m_t))
        sl = slice(h * HEAD_DIM, (h + 1) * HEAD_DIM)
        hn = hh * lax.rsqrt(jnp.mean(hh * hh, axis=-1, keepdims=True) + EPS) * ng_ref[:, sl]
        gate = _sigmoid(o_ref[:, sl].astype(F32)) * _sigmoid(ga_ref[:, sl].astype(F32) + bg_ref[:, sl])
        out_ref[:, sl] = (gate * hn).astype(out_ref.dtype)

        a_t = b_last - b_t + i_t
        m_loc = jnp.max(a_t, axis=0, keepdims=True)
        w_t = jnp.exp(a_t - m_loc)
        wv = (w_t * vh.astype(F32)).astype(BF16)
        c_loc = lax.dot_general(km_b, wv, (((0,), (0,)), ((), ())), preferred_element_type=F32)
        n_loc = jnp.sum(w_t * km, axis=0, keepdims=True)
        m_new = jnp.maximum(b_last + m0, m_loc)
        sp = jnp.exp(b_last + m0 - m_new)
        sc = jnp.exp(m_loc - m_new)
        c_ref[h] = sp * c0 + sc * c_loc
        n_ref[h] = sp * n0 + sc * n_loc
        m_ref[h] = jnp.broadcast_to(m_new, (1, LANES))


def _mlstm(p, gates, gates_t, conv_w, bif_row, bif_col, norm_g, b_gate_a, B, S, L):
    T = B * S
    D = N_HEADS * HEAD_DIM
    nc = S // L
    tok = lambda blk: pl.BlockSpec((L, D), lambda b, c: (b * nc + c, blk))
    const = lambda shape: pl.BlockSpec(shape, lambda b, c: (0,) * len(shape))
    return pl.pallas_call(
        functools.partial(_mlstm_kernel, L=L),
        grid=(B, nc),
        in_specs=[
            tok(BLK_QK_A), tok(BLK_V_A), tok(BLK_O_A), tok(BLK_G_A),
            pl.BlockSpec((L, LANES), lambda b, c: (b * nc + c, 0)),
            pl.BlockSpec((1, 2 * N_HEADS, L), lambda b, c: (b * nc + c, 0, 0)),
            const((CONV_K, D)), const((1, LANES)), const((2 * N_HEADS, 1)), const((1, D)), const((1, D)),
        ],
        out_specs=pl.BlockSpec((L, D), lambda b, c: (b * nc + c, 0)),
        out_shape=jax.ShapeDtypeStruct((T, D), BF16),
        scratch_shapes=[
            pltpu.VMEM((L + CONV_HALO, D), F32),
            pltpu.VMEM((N_HEADS, LANES, HEAD_DIM), F32),
            pltpu.VMEM((N_HEADS, 1, LANES), F32),
            pltpu.VMEM((N_HEADS, 1, LANES), F32),
        ],
        compiler_params=pltpu.CompilerParams(
            dimension_semantics=("parallel", "arbitrary"), vmem_limit_bytes=VMEM_LIMIT),
        name="mlstm",
    )(p, p, p, p, gates, gates_t, conv_w, bif_row, bif_col, norm_g, b_gate_a)


def _sb_kernel(q_ref, k_ref, v_ref, gb_ref, ya_ref, bg_ref, csm_ref, out_ref, acc_ref, carry_ref, *, tq, hp):
    i = pl.program_id(2)
    csm = csm_ref[...]

    def cols(h):
        return slice(h * HEAD_DIM, (h + 1) * HEAD_DIM)

    def keys(h, j, n):
        return k_ref[pl.ds(pl.multiple_of(j * tq, tq), n * tq), cols(h)]

    def values(h, j, n):
        return v_ref[pl.ds(pl.multiple_of(j * tq, tq), n * tq), cols(h)]

    def scores(h, kj):
        z = lax.dot_general(q_ref[:, cols(h)], kj, (((1,), (1,)), ((), ())), preferred_element_type=F32)
        return z, jnp.maximum(z, 0.0) + jnp.log2(1.0 + jnp.exp2(-jnp.abs(z)))

    def suffix(nlk):
        return jnp.dot(nlk.astype(BF16), csm, preferred_element_type=F32)

    def total(nlk):
        return -jnp.sum(nlk, axis=1, keepdims=True)

    def strict_mask():
        row = lax.broadcasted_iota(jnp.int32, (tq, tq), 0)
        col = lax.broadcasted_iota(jnp.int32, (tq, tq), 1)
        return col < row

    @pl.when(i == 0)
    def _():
        strict = strict_mask()
        for h in range(hp):
            z, nlk = scores(h, keys(h, 0, 1))
            nlk = jnp.where(strict, nlk, 0.0)
            a = jnp.where(strict, jnp.exp2(z - nlk + suffix(nlk)), 0.0)
            acc_ref[h] = jnp.dot(a.astype(BF16), values(h, 0, 1), preferred_element_type=F32)

    @pl.when(i > 0)
    def _():
        strict = strict_mask()
        alive = []
        for h in range(hp):
            z, nlk = scores(h, keys(h, i - 1, 2))
            z_d, z_p = z[:, tq:], z[:, :tq]
            nlk_d, nlk_p = jnp.where(strict, nlk[:, tq:], 0.0), nlk[:, :tq]
            a_d = jnp.where(strict, jnp.exp2(z_d - nlk_d + suffix(nlk_d)), 0.0)
            carry = total(nlk_d)
            a_p = jnp.exp2(z_p - nlk_p + suffix(nlk_p) + carry)
            a = jnp.concatenate([a_p, a_d], axis=1).astype(BF16)
            acc_ref[h] = jnp.dot(a, values(h, i - 1, 2), preferred_element_type=F32)
            carry = carry + total(nlk_p)
            carry_ref[h] = jnp.broadcast_to(carry, (tq, LANES))
            alive.append((jnp.max(carry) > UNDERFLOW_LOG2).astype(jnp.int32))

        def cond(state):
            j, alive_h = state
            return jnp.logical_and(j >= 0, alive_h > 0)

        for h in range(hp):
            def body(state, h=h):
                j, _ = state
                z, nlk = scores(h, keys(h, j, 1))
                carry = carry_ref[h][:, 0:1]
                a = jnp.exp2(z - nlk + suffix(nlk) + carry)
                acc_ref[h] += jnp.dot(a.astype(BF16), values(h, j, 1), preferred_element_type=F32)
                carry = carry + total(nlk)
                carry_ref[h] = jnp.broadcast_to(carry, (tq, LANES))
                return j - 1, (jnp.max(carry) > UNDERFLOW_LOG2).astype(jnp.int32)

            lax.while_loop(cond, body, (i - 2, alive[h]))

    for h in range(hp):
        gate = _sigmoid(gb_ref[:, cols(h)].astype(F32) + bg_ref[:, cols(h)])
        out_ref[:, cols(h)] = (ya_ref[:, cols(h)].astype(F32) + gate * acc_ref[h]).astype(out_ref.dtype)


def _stick_breaking(p, ya, b_gate_b, csm, B, S, tq, hp):
    T = B * S
    D = N_HEADS * HEAD_DIM
    nq = S // tq
    w = hp * HEAD_DIM
    groups = N_HEADS // hp
    tile = lambda blk: pl.BlockSpec((tq, w), lambda b, g, i: (b * nq + i, blk * groups + g))
    seq = lambda blk: pl.BlockSpec((S, w), lambda b, g, i: (b, blk * groups + g))
    return pl.pallas_call(
        functools.partial(_sb_kernel, tq=tq, hp=hp),
        grid=(B, groups, nq),
        in_specs=[
            tile(BLK_Q_B), seq(BLK_K_B), seq(BLK_V_B), tile(BLK_G_B),
            pl.BlockSpec((tq, w), lambda b, g, i: (b * nq + i, g)),
            pl.BlockSpec((1, w), lambda b, g, i: (0, g)),
            pl.BlockSpec((tq, tq), lambda b, g, i: (0, 0)),
        ],
        out_specs=pl.BlockSpec((tq, w), lambda b, g, i: (b * nq + i, g)),
        out_shape=jax.ShapeDtypeStruct((T, D), BF16),
        scratch_shapes=[pltpu.VMEM((hp, tq, HEAD_DIM), F32), pltpu.VMEM((hp, tq, LANES), F32)],
        compiler_params=pltpu.CompilerParams(
            dimension_semantics=("parallel", "parallel", "arbitrary"), vmem_limit_bytes=VMEM_LIMIT),
        name="stick_breaking",
    )(p, p, p, p, ya, b_gate_b, csm)


def _mlp_kernel(x_ref, y_ref, wo_ref, g_ref, wu_ref, wd_ref, out_ref, h_ref):
    f = pl.program_id(1)

    @pl.when(f == 0)
    def _():
        x1 = x_ref[...] + jnp.dot(y_ref[...], wo_ref[...], preferred_element_type=F32)
        out_ref[...] = x1
        ms = jnp.mean(x1 * x1, axis=-1, keepdims=True)
        h_ref[...] = (x1 * lax.rsqrt(ms + EPS) * g_ref[...]).astype(BF16)

    u = jnp.maximum(jnp.dot(h_ref[...], wu_ref[...], preferred_element_type=F32), 0.0)
    out_ref[...] += jnp.dot((u * u).astype(BF16), wd_ref[...], preferred_element_type=F32)


def _outproj_mlp(x2d, y, w_out, g, w_up, w_down, tm, tf):
    T, D = x2d.shape
    F = w_up.shape[1]
    return pl.pallas_call(
        _mlp_kernel,
        grid=(T // tm, F // tf),
        in_specs=[
            pl.BlockSpec((tm, D), lambda i, f: (i, 0)),
            pl.BlockSpec((tm, D), lambda i, f: (i, 0)),
            pl.BlockSpec((D, D), lambda i, f: (0, 0)),
            pl.BlockSpec((1, D), lambda i, f: (0, 0)),
            pl.BlockSpec((D, tf), lambda i, f: (0, f)),
            pl.BlockSpec((tf, D), lambda i, f: (f, 0)),
        ],
        out_specs=pl.BlockSpec((tm, D), lambda i, f: (i, 0)),
        out_shape=jax.ShapeDtypeStruct((T, D), F32),
        scratch_shapes=[pltpu.VMEM((tm, D), BF16)],
        compiler_params=pltpu.CompilerParams(
            dimension_semantics=("parallel", "arbitrary"), vmem_limit_bytes=VMEM_LIMIT),
        name="outproj_mlp",
    )(x2d, y, w_out, g, w_up, w_down)


def _tiles(B, S):
    T = B * S
    return dict(
        tm_in=min(1024, T),
        chunk=min(256, S),
        tq=min(256, S),
        sb_heads=4,
        tm_mlp=min(1024, T),
        tf=1024,
    )


def kernel(x, norm_mix_g, w_in, b_if, b_gate, conv_w, mlstm_norm_g, sb_q_norm_g, sb_k_norm_g,
           w_out, norm_mlp_g, w_up, w_down):
    B, S, D = x.shape
    depth = w_in.shape[0]
    T = B * S
    assert D == N_HEADS * HEAD_DIM and w_in.shape[2] == 8 * D + 2 * N_HEADS
    cfg = _tiles(B, S)
    tq = cfg["tq"]
    gate_lo = 3 * D
    gate_hi = gate_lo + 2 * N_HEADS

    r = lax.broadcasted_iota(jnp.int32, (tq, tq), 0)
    c = lax.broadcasted_iota(jnp.int32, (tq, tq), 1)
    csm = jnp.where(r > c, -1.0, 0.0).astype(BF16)

    x2d = x.reshape(T, D)
    for l in range(depth):
        w_nogate = jnp.concatenate([w_in[l][:, :gate_lo], w_in[l][:, gate_hi:]], axis=1)
        src = {blk: pos for pos, blk in enumerate(W_IN_ORDER)}
        w_main = jnp.concatenate([w_nogate[:, src[b] * D:(src[b] + 1) * D] for b in range(N_BLK)],
                                 axis=1).astype(BF16)
        w_gate = jnp.pad(w_in[l][:, gate_lo:gate_hi], ((0, 0), (0, LANES - 2 * N_HEADS))).astype(BF16)
        w_gate_t = w_in[l][:, gate_lo:gate_hi].T.astype(BF16)
        qk_gain = jnp.concatenate([jnp.tile(sb_q_norm_g[l], N_HEADS),
                                   jnp.tile(sb_k_norm_g[l] * (HEAD_DIM ** -0.5 * LOG2_E), N_HEADS)]
                                  ).reshape(1, 2 * D)
        p, gates, gates_t = _inproj(x2d, norm_mix_g[l].reshape(1, D), w_main, w_gate, w_gate_t,
                                    qk_gain, cfg["tm_in"])
        bif_row = jnp.pad(b_if[l], (0, LANES - 2 * N_HEADS)).reshape(1, LANES)
        bif_col = b_if[l].reshape(2 * N_HEADS, 1)
        L = cfg["chunk"]
        gates_t = gates_t.reshape(2 * N_HEADS, T // L, L).transpose(1, 0, 2)
        ya = _mlstm(p, gates, gates_t, conv_w[l], bif_row, bif_col, mlstm_norm_g[l].reshape(1, D),
                    b_gate[l][:D].reshape(1, D), B, S, cfg["chunk"])
        y = _stick_breaking(p, ya, b_gate[l][D:].reshape(1, D), csm, B, S, tq, cfg["sb_heads"])
        x2d = _outproj_mlp(x2d, y, w_out[l].astype(BF16), norm_mlp_g[l].reshape(1, D),
                           w_up[l].astype(BF16), w_down[l].astype(BF16), cfg["tm_mlp"], cfg["tf"])
    return x2d.reshape(B, S, D)
```

```python
import functools

import jax
import jax.numpy as jnp
from jax import lax
from jax.experimental import pallas as pl
from jax.experimental.pallas import tpu as pltpu

F32 = jnp.float32
BF16 = jnp.bfloat16

EPS = 1e-6
N_HEADS = 8
HEAD_DIM = 128
DQK_A = 64
CONV_K = 4
LANES = 128
CONV_HALO = 8
VMEM_LIMIT = 56 * 1024 * 1024
LOG2_E = 1.4426950408889634
UNDERFLOW_LOG2 = -110.0 * LOG2_E

BLK_QK_A, BLK_V_A, BLK_O_A, BLK_G_A, BLK_G_B, BLK_V_B, BLK_Q_B, BLK_K_B = range(8)
W_IN_ORDER = (BLK_QK_A, BLK_V_A, BLK_O_A, BLK_Q_B, BLK_K_B, BLK_V_B, BLK_G_A, BLK_G_B)
N_BLK = 8
INPROJ_TN = 2


def _log_sigmoid(x):
    return jnp.minimum(x, 0.0) - jnp.log(1.0 + jnp.exp2(jnp.abs(x) * -LOG2_E))


def _exp_neg(x):
    return jnp.exp2(x * -LOG2_E)


def _sigmoid(x):
    return 1.0 / (1.0 + _exp_neg(x))


def _inproj_kernel(x_ref, g_ref, w_ref, wg_ref, wgt_ref, qkg_ref,
                   p_ref, gates_ref, gates_t_ref, h_ref):
    j = pl.program_id(1)

    @pl.when(j == 0)
    def _():
        x = x_ref[...]
        ms = jnp.mean(x * x, axis=-1, keepdims=True)
        h = (x * lax.rsqrt(ms + EPS) * g_ref[...]).astype(BF16)
        h_ref[...] = h
        gates_ref[...] = jnp.dot(h, wg_ref[...], preferred_element_type=F32)
        gates_t_ref[...] = lax.dot_general(wgt_ref[...], h, (((1,), (1,)), ((), ())),
                                           preferred_element_type=F32)

    is_norm = j == pl.num_programs(1) - 1

    @pl.when(jnp.logical_not(is_norm))
    def _():
        acc = jnp.dot(h_ref[...], w_ref[j], preferred_element_type=F32)
        p_ref[...] = acc.astype(p_ref.dtype)

    @pl.when(is_norm)
    def _():
        acc = jnp.dot(h_ref[...], w_ref[j], preferred_element_type=F32)
        gain = qkg_ref[...]
        for h in range(2 * N_HEADS):
            sl = slice(h * HEAD_DIM, (h + 1) * HEAD_DIM)
            a = acc[:, sl]
            ms = jnp.mean(a * a, axis=-1, keepdims=True)
            p_ref[:, sl] = (a * lax.rsqrt(ms + EPS) * gain[:, sl]).astype(p_ref.dtype)


def _inproj(x2d, g, w_main, w_gate, w_gate_t, qk_gain, tm):
    T, D = x2d.shape
    n_tiles, _, tn = w_main.shape
    n_cols = n_tiles * tn
    assert (BLK_Q_B, BLK_K_B) == (N_BLK - 2, N_BLK - 1) and qk_gain.shape == (1, tn)
    grid = (T // tm, n_tiles)
    return pl.pallas_call(
        _inproj_kernel,
        grid=grid,
        in_specs=[
            pl.BlockSpec((tm, D), lambda i, j: (i, 0)),
            pl.BlockSpec((1, D), lambda i, j: (0, 0)),
            pl.BlockSpec((n_tiles, D, tn), lambda i, j: (0, 0, 0), pipeline_mode=pl.Buffered(1)),
            pl.BlockSpec((D, LANES), lambda i, j: (0, 0)),
            pl.BlockSpec((2 * N_HEADS, D), lambda i, j: (0, 0)),
            pl.BlockSpec((1, tn), lambda i, j: (0, 0)),
        ],
        out_specs=[
            pl.BlockSpec((tm, tn), lambda i, j: (i, j)),
            pl.BlockSpec((tm, LANES), lambda i, j: (i, 0)),
            pl.BlockSpec((2 * N_HEADS, tm), lambda i, j: (0, i)),
        ],
        out_shape=[
            jax.ShapeDtypeStruct((T, n_cols), BF16),
            jax.ShapeDtypeStruct((T, LANES), F32),
            jax.ShapeDtypeStruct((2 * N_HEADS, T), F32),
        ],
        scratch_shapes=[pltpu.VMEM((tm, D), BF16)],
        compiler_params=pltpu.CompilerParams(
            dimension_semantics=("parallel", "arbitrary"), vmem_limit_bytes=VMEM_LIMIT),
        name="norm_inproj",
    )(x2d, g, w_main, w_gate, w_gate_t, qk_gain)


def _mlstm_kernel(qk_ref, v_ref, o_ref, ga_ref, gates_ref, gates_t_ref,
                  convw_ref, bif_row_ref, bif_col_ref, ng_ref, bg_ref,
                  out_ref, xbuf_ref, c_ref, n_ref, m_ref, *, L):
    c = pl.program_id(1)
    d_qk = N_HEADS * DQK_A

    @pl.when(c == 0)
    def _():
        xbuf_ref[0:CONV_HALO, :] = jnp.zeros((CONV_HALO, xbuf_ref.shape[1]), F32)
        c_ref[...] = jnp.zeros(c_ref.shape, F32)
        n_ref[...] = jnp.zeros(n_ref.shape, F32)
        m_ref[...] = jnp.zeros(m_ref.shape, F32)

    xbuf_ref[CONV_HALO:CONV_HALO + L, :] = qk_ref[...].astype(F32)
    w = convw_ref[...]
    y = w[CONV_K - 1:CONV_K, :] * xbuf_ref[CONV_HALO:CONV_HALO + L, :]
    for tap in range(CONV_K - 1):
        off = CONV_HALO - (CONV_K - 1) + tap
        y = y + w[tap:tap + 1, :] * xbuf_ref[off:off + L, :]
    xbuf_ref[0:CONV_HALO, :] = xbuf_ref[L:L + CONV_HALO, :]
    qk = y * _sigmoid(y)

    row = lax.broadcasted_iota(jnp.int32, (L, L), 0)
    col = lax.broadcasted_iota(jnp.int32, (L, L), 1)
    causal = col <= row
    tri = jnp.where(causal, 1.0, 0.0).astype(F32)
    pre_col = gates_ref[...] + bif_row_ref[...]
    b_col_all = jnp.dot(tri, _log_sigmoid(pre_col), preferred_element_type=F32,
                        precision=lax.Precision.HIGHEST)
    pre_row = gates_t_ref[0] + bif_col_ref[...]
    b_row_all = lax.dot_general(_log_sigmoid(pre_row), tri, (((1,), (1,)), ((), ())),
                                preferred_element_type=F32, precision=lax.Precision.HIGHEST)

    lane = lax.broadcasted_iota(jnp.int32, (L, LANES), 1)
    for h in range(N_HEADS):
        pair, half = divmod(h, 2)
        q_pair = qk[:, pair * LANES:(pair + 1) * LANES].astype(BF16)
        k_pair = qk[:, d_qk + pair * LANES:d_qk + (pair + 1) * LANES]
        in_head = jnp.logical_and(lane >= half * DQK_A, lane < (half + 1) * DQK_A)
        km = jnp.where(in_head, k_pair * (DQK_A ** -0.5), 0.0)
        km_b = km.astype(BF16)
        vh = v_ref[:, h * HEAD_DIM:(h + 1) * HEAD_DIM]

        b_t = b_col_all[:, N_HEADS + h:N_HEADS + h + 1]
        i_t = pre_col[:, h:h + 1]
        b_s = b_row_all[N_HEADS + h:N_HEADS + h + 1, :]
        i_s = pre_row[h:h + 1, :]
        b_last = b_t[L - 1:L, :]
        m0 = m_ref[h][:, 0:1]
        c0 = c_ref[h]
        n0 = n_ref[h]

        dmat = jnp.where(causal, b_t - b_s + i_s, -jnp.inf)
        b_inter = b_t + m0
        m_t = jnp.maximum(b_inter, jnp.max(dmat, axis=-1, keepdims=True))
        s_qk = lax.dot_general(q_pair, km_b, (((1,), (1,)), ((), ())), preferred_element_type=F32)
        p = jnp.exp(dmat - m_t) * s_qk
        inter = jnp.exp(b_inter - m_t)
        num = (jnp.dot(p.astype(BF16), vh, preferred_element_type=F32)
               + inter * jnp.dot(q_pair, c0.astype(BF16), preferred_element_type=F32))
        den = (jnp.sum(p, axis=-1, keepdims=True)
               + inter * jnp.sum(q_pair.astype(F32) * n0, axis=-1, keepdims=True))
        hh = num * (1.0 / jnp.maximum(jnp.abs(den), _exp_neg(m_t)))
        sl = slice(h * HEAD_DIM, (h + 1) * HEAD_DIM)
        hn = hh * lax.rsqrt(jnp.mean(hh * hh, axis=-1, keepdims=True) + EPS) * ng_ref[:, sl]
        gate = 1.0 / ((1.0 + _exp_neg(o_ref[:, sl].astype(F32)))
                      * (1.0 + _exp_neg(ga_ref[:, sl].astype(F32) + bg_ref[:, sl])))
        out_ref[:, sl] = (gate * hn).astype(out_ref.dtype)

        a_t = b_last - b_t + i_t
        m_loc = jnp.max(a_t, axis=0, keepdims=True)
        w_t = jnp.exp(a_t - m_loc)
        wv = (w_t * vh.astype(F32)).astype(BF16)
        c_loc = lax.dot_general(km_b, wv, (((0,), (0,)), ((), ())), preferred_element_type=F32)
        n_loc = jnp.sum(w_t * km, axis=0, keepdims=True)
        m_new = jnp.maximum(b_last + m0, m_loc)
        sp = jnp.exp(b_last + m0 - m_new)
        sc = jnp.exp(m_loc - m_new)
        c_ref[h] = sp * c0 + sc * c_loc
        n_ref[h] = sp * n0 + sc * n_loc
        m_ref[h] = jnp.broadcast_to(m_new, (1, LANES))


def _mlstm(p, gates, gates_t, conv_w, bif_row, bif_col, norm_g, b_gate_a, B, S, L):
    T = B * S
    D = N_HEADS * HEAD_DIM
    nc = S // L
    tok = lambda blk: pl.BlockSpec((L, D), lambda b, c: (b * nc + c, blk))
    const = lambda shape: pl.BlockSpec(shape, lambda b, c: (0,) * len(shape))
    return pl.pallas_call(
        functools.partial(_mlstm_kernel, L=L),
        grid=(B, nc),
        in_specs=[
            tok(BLK_QK_A), tok(BLK_V_A), tok(BLK_O_A), tok(BLK_G_A),
            pl.BlockSpec((L, LANES), lambda b, c: (b * nc + c, 0)),
            pl.BlockSpec((1, 2 * N_HEADS, L), lambda b, c: (b * nc + c, 0, 0)),
            const((CONV_K, D)), const((1, LANES)), const((2 * N_HEADS, 1)), const((1, D)), const((1, D)),
        ],
        out_specs=pl.BlockSpec((L, D), lambda b, c: (b * nc + c, 0)),
        out_shape=jax.ShapeDtypeStruct((T, D), BF16),
        scratch_shapes=[
            pltpu.VMEM((L + CONV_HALO, D), F32),
            pltpu.VMEM((N_HEADS, LANES, HEAD_DIM), F32),
            pltpu.VMEM((N_HEADS, 1, LANES), F32),
            pltpu.VMEM((N_HEADS, 1, LANES), F32),
        ],
        compiler_params=pltpu.CompilerParams(
            dimension_semantics=("parallel", "arbitrary"), vmem_limit_bytes=VMEM_LIMIT),
        name="mlstm",
    )(p, p, p, p, gates, gates_t, conv_w, bif_row, bif_col, norm_g, b_gate_a)


def _sb_kernel(q_ref, k_ref, v_ref, gb_ref, ya_ref, bg_ref, csm_ref, out_ref, acc_ref, carry_ref, *, tq, hp):
    i = pl.program_id(2)
    csm = csm_ref[...]

    def cols(h):
        return slice(h * HEAD_DIM, (h + 1) * HEAD_DIM)

    def keys(h, j, n):
        return k_ref[pl.ds(pl.multiple_of(j * tq, tq), n * tq), cols(h)]

    def values(h, j, n):
        return v_ref[pl.ds(pl.multiple_of(j * tq, tq), n * tq), cols(h)]

    def scores(h, kj):
        z = lax.dot_general(q_ref[:, cols(h)], kj, (((1,), (1,)), ((), ())), preferred_element_type=F32)
        return z, jnp.maximum(z, 0.0) + jnp.log2(1.0 + jnp.exp2(-jnp.abs(z)))

    def suffix(nlk):
        return jnp.dot(nlk.astype(BF16), csm, preferred_element_type=F32)

    def total(nlk):
        return -jnp.sum(nlk, axis=1, keepdims=True)

    def strict_mask():
        row = lax.broadcasted_iota(jnp.int32, (tq, tq), 0)
        col = lax.broadcasted_iota(jnp.int32, (tq, tq), 1)
        return col < row

    @pl.when(i == 0)
    def _():
        strict = strict_mask()
        for h in range(hp):
            z, nlk = scores(h, keys(h, 0, 1))
            nlk = jnp.where(strict, nlk, 0.0)
            a = jnp.where(strict, jnp.exp2(z - nlk + suffix(nlk)), 0.0)
            acc_ref[h] = jnp.dot(a.astype(BF16), values(h, 0, 1), preferred_element_type=F32)

    @pl.when(i > 0)
    def _():
        strict = strict_mask()
        alive = []
        for h in range(hp):
            z, nlk = scores(h, keys(h, i - 1, 2))
            z_d, z_p = z[:, tq:], z[:, :tq]
            nlk_d, nlk_p = jnp.where(strict, nlk[:, tq:], 0.0), nlk[:, :tq]
            a_d = jnp.where(strict, jnp.exp2(z_d - nlk_d + suffix(nlk_d)), 0.0)
            carry = total(nlk_d)
            a_p = jnp.exp2(z_p - nlk_p + suffix(nlk_p) + carry)
            a = jnp.concatenate([a_p, a_d], axis=1).astype(BF16)
            acc_ref[h] = jnp.dot(a, values(h, i - 1, 2), preferred_element_type=F32)
            carry = carry + total(nlk_p)
            carry_ref[h] = jnp.broadcast_to(carry, (tq, LANES))
            alive.append((jnp.max(carry) > UNDERFLOW_LOG2).astype(jnp.int32))

        def cond(state):
            j, alive_h = state
            return jnp.logical_and(j >= 0, alive_h > 0)

        for h in range(hp):
            def body(state, h=h):
                j, _ = state
                z, nlk = scores(h, keys(h, j, 1))
                carry = carry_ref[h][:, 0:1]
                a = jnp.exp2(z - nlk + suffix(nlk) + carry)
                acc_ref[h] += jnp.dot(a.astype(BF16), values(h, j, 1), preferred_element_type=F32)
                carry = carry + total(nlk)
                carry_ref[h] = jnp.broadcast_to(carry, (tq, LANES))
                return j - 1, (jnp.max(carry) > UNDERFLOW_LOG2).astype(jnp.int32)

            lax.while_loop(cond, body, (i - 2, alive[h]))

    for h in range(hp):
        gate = _sigmoid(gb_ref[:, cols(h)].astype(F32) + bg_ref[:, cols(h)])
        out_ref[:, cols(h)] = (ya_ref[:, cols(h)].astype(F32) + gate * acc_ref[h]).astype(out_ref.dtype)


def _stick_breaking(p, ya, b_gate_b, csm, B, S, tq, hp):
    T = B * S
    D = N_HEADS * HEAD_DIM
    nq = S // tq
    w = hp * HEAD_DIM
    groups = N_HEADS // hp
    tile = lambda blk: pl.BlockSpec((tq, w), lambda b, g, i: (b * nq + i, blk * groups + g))
    seq = lambda blk: pl.BlockSpec((S, w), lambda b, g, i: (b, blk * groups + g))
    return pl.pallas_call(
        functools.partial(_sb_kernel, tq=tq, hp=hp),
        grid=(B, groups, nq),
        in_specs=[
            tile(BLK_Q_B), seq(BLK_K_B), seq(BLK_V_B), tile(BLK_G_B),
            pl.BlockSpec((tq, w), lambda b, g, i: (b * nq + i, g)),
            pl.BlockSpec((1, w), lambda b, g, i: (0, g)),
            pl.BlockSpec((tq, tq), lambda b, g, i: (0, 0)),
        ],
        out_specs=pl.BlockSpec((tq, w), lambda b, g, i: (b * nq + i, g)),
        out_shape=jax.ShapeDtypeStruct((T, D), BF16),
        scratch_shapes=[pltpu.VMEM((hp, tq, HEAD_DIM), F32), pltpu.VMEM((hp, tq, LANES), F32)],
        compiler_params=pltpu.CompilerParams(
            dimension_semantics=("parallel", "parallel", "arbitrary"), vmem_limit_bytes=VMEM_LIMIT),
        name="stick_breaking",
    )(p, p, p, p, ya, b_gate_b, csm)


def _mlp_kernel(x_ref, y_ref, wo_ref, g_ref, wu_ref, wd_ref, out_ref, h_ref):
    f = pl.program_id(1)

    @pl.when(f == 0)
    def _():
        x1 = x_ref[...] + jnp.dot(y_ref[...], wo_ref[...], preferred_element_type=F32)
        out_ref[...] = x1
        ms = jnp.mean(x1 * x1, axis=-1, keepdims=True)
        h_ref[...] = (x1 * lax.rsqrt(ms + EPS) * g_ref[...]).astype(BF16)

    u = jnp.maximum(jnp.dot(h_ref[...], wu_ref[f], preferred_element_type=F32), 0.0)
    out_ref[...] += jnp.dot((u * u).astype(BF16), wd_ref[f], preferred_element_type=F32)


def _outproj_mlp(x2d, y, w_out, g, w_up, w_down, tm):
    T, D = x2d.shape
    n_f, _, tf = w_up.shape
    resident = lambda shape: pl.BlockSpec(shape, lambda i, f: (0,) * len(shape), pipeline_mode=pl.Buffered(1))
    return pl.pallas_call(
        _mlp_kernel,
        grid=(T // tm, n_f),
        in_specs=[
            pl.BlockSpec((tm, D), lambda i, f: (i, 0)),
            pl.BlockSpec((tm, D), lambda i, f: (i, 0)),
            resident((D, D)),
            pl.BlockSpec((1, D), lambda i, f: (0, 0)),
            resident((n_f, D, tf)),
            resident((n_f, tf, D)),
        ],
        out_specs=pl.BlockSpec((tm, D), lambda i, f: (i, 0)),
        out_shape=jax.ShapeDtypeStruct((T, D), F32),
        scratch_shapes=[pltpu.VMEM((tm, D), BF16)],
        compiler_params=pltpu.CompilerParams(
            dimension_semantics=("parallel", "arbitrary"), vmem_limit_bytes=VMEM_LIMIT),
        name="outproj_mlp",
    )(x2d, y, w_out, g, w_up, w_down)


def _tiles(B, S):
    T = B * S
    return dict(
        tm_in=min(1024, T),
        chunk=min(256, S),
        tq=min(256, S),
        sb_heads=4,
        tm_mlp=min(1024, T),
        tf=1024,
    )


def kernel(x, norm_mix_g, w_in, b_if, b_gate, conv_w, mlstm_norm_g, sb_q_norm_g, sb_k_norm_g,
           w_out, norm_mlp_g, w_up, w_down):
    B, S, D = x.shape
    depth = w_in.shape[0]
    T = B * S
    assert D == N_HEADS * HEAD_DIM and w_in.shape[2] == 8 * D + 2 * N_HEADS
    cfg = _tiles(B, S)
    tq = cfg["tq"]
    gate_lo = 3 * D
    gate_hi = gate_lo + 2 * N_HEADS

    r = lax.broadcasted_iota(jnp.int32, (tq, tq), 0)
    c = lax.broadcasted_iota(jnp.int32, (tq, tq), 1)
    csm = jnp.where(r > c, -1.0, 0.0).astype(BF16)

    x2d = x.reshape(T, D)
    for l in range(depth):
        w_nogate = jnp.concatenate([w_in[l][:, :gate_lo], w_in[l][:, gate_hi:]], axis=1)
        src = {blk: pos for pos, blk in enumerate(W_IN_ORDER)}
        w_main = jnp.stack([jnp.concatenate([w_nogate[:, src[b] * D:(src[b] + 1) * D]
                                             for b in range(t * INPROJ_TN, (t + 1) * INPROJ_TN)], axis=1)
                            for t in range(N_BLK // INPROJ_TN)]).astype(BF16)
        w_gate = jnp.pad(w_in[l][:, gate_lo:gate_hi], ((0, 0), (0, LANES - 2 * N_HEADS))).astype(BF16)
        w_gate_t = w_in[l][:, gate_lo:gate_hi].T.astype(BF16)
        qk_gain = jnp.concatenate([jnp.tile(sb_q_norm_g[l], N_HEADS),
                                   jnp.tile(sb_k_norm_g[l] * (HEAD_DIM ** -0.5 * LOG2_E), N_HEADS)]
                                  ).reshape(1, 2 * D)
        p, gates, gates_t = _inproj(x2d, norm_mix_g[l].reshape(1, D), w_main, w_gate, w_gate_t,
                                    qk_gain, cfg["tm_in"])
        bif_row = jnp.pad(b_if[l], (0, LANES - 2 * N_HEADS)).reshape(1, LANES)
        bif_col = b_if[l].reshape(2 * N_HEADS, 1)
        L = cfg["chunk"]
        gates_t = gates_t.reshape(2 * N_HEADS, T // L, L).transpose(1, 0, 2)
        ya = _mlstm(p, gates, gates_t, conv_w[l], bif_row, bif_col, mlstm_norm_g[l].reshape(1, D),
                    b_gate[l][:D].reshape(1, D), B, S, cfg["chunk"])
        y = _stick_breaking(p, ya, b_gate[l][D:].reshape(1, D), csm, B, S, tq, cfg["sb_heads"])
        tf = cfg["tf"]
        n_f = w_up.shape[2] // tf
        w_up_t = w_up[l].astype(BF16).reshape(D, n_f, tf).transpose(1, 0, 2)
        w_down_t = w_down[l].astype(BF16).reshape(n_f, tf, D)
        x2d = _outproj_mlp(x2d, y, w_out[l].astype(BF16), norm_mlp_g[l].reshape(1, D),
                           w_up_t, w_down_t, cfg["tm_mlp"])
    return x2d.reshape(B, S, D)
```

```python
import functools

import jax
import jax.numpy as jnp
from jax import lax
from jax.experimental import pallas as pl
from jax.experimental.pallas import tpu as pltpu

F32 = jnp.float32
BF16 = jnp.bfloat16

EPS = 1e-6
N_HEADS = 8
HEAD_DIM = 128
DQK_A = 64
CONV_K = 4
LANES = 128
CONV_HALO = 8
VMEM_LIMIT = 56 * 1024 * 1024
LOG2_E = 1.4426950408889634
UNDERFLOW_LOG2 = -110.0 * LOG2_E

BLK_QK_A, BLK_V_A, BLK_O_A, BLK_G_A, BLK_G_B, BLK_V_B, BLK_Q_B, BLK_K_B = range(8)
W_IN_ORDER = (BLK_QK_A, BLK_V_A, BLK_O_A, BLK_Q_B, BLK_K_B, BLK_V_B, BLK_G_A, BLK_G_B)
N_BLK = 8
SB_STAGE_HEADS = 2
INPROJ_TN = 2


def _log_sigmoid(x):
    return jnp.minimum(x, 0.0) - jnp.log(1.0 + jnp.exp2(jnp.abs(x) * -LOG2_E))


def _exp_neg(x):
    return jnp.exp2(x * -LOG2_E)


def _sigmoid(x):
    return 1.0 / (1.0 + _exp_neg(x))


def _inproj_kernel(x_ref, g_ref, w_ref, wg_ref, wgt_ref, qkg_ref,
                   p_ref, gates_ref, gates_t_ref, h_ref):
    j = pl.program_id(1)

    @pl.when(j == 0)
    def _():
        x = x_ref[...]
        ms = jnp.mean(x * x, axis=-1, keepdims=True)
        h = (x * lax.rsqrt(ms + EPS) * g_ref[...]).astype(BF16)
        h_ref[...] = h
        gates_ref[...] = jnp.dot(h, wg_ref[...], preferred_element_type=F32)
        gates_t_ref[...] = lax.dot_general(wgt_ref[...], h, (((1,), (1,)), ((), ())),
                                           preferred_element_type=F32)

    is_norm = j == pl.num_programs(1) - 1

    @pl.when(jnp.logical_not(is_norm))
    def _():
        acc = jnp.dot(h_ref[...], w_ref[...], preferred_element_type=F32)
        p_ref[...] = acc.astype(p_ref.dtype)

    @pl.when(is_norm)
    def _():
        acc = jnp.dot(h_ref[...], w_ref[...], preferred_element_type=F32)
        gain = qkg_ref[...]
        for h in range(2 * N_HEADS):
            sl = slice(h * HEAD_DIM, (h + 1) * HEAD_DIM)
            a = acc[:, sl]
            ms = jnp.mean(a * a, axis=-1, keepdims=True)
            p_ref[:, sl] = (a * lax.rsqrt(ms + EPS) * gain[:, sl]).astype(p_ref.dtype)


def _inproj(x2d, g, w_main, w_gate, w_gate_t, qk_gain, tm):
    T, D = x2d.shape
    n_cols = w_main.shape[1]
    tn = INPROJ_TN * D
    assert (BLK_Q_B, BLK_K_B) == (N_BLK - 2, N_BLK - 1) and qk_gain.shape == (1, tn)
    grid = (T // tm, n_cols // tn)
    return pl.pallas_call(
        _inproj_kernel,
        grid=grid,
        in_specs=[
            pl.BlockSpec((tm, D), lambda i, j: (i, 0)),
            pl.BlockSpec((1, D), lambda i, j: (0, 0)),
            pl.BlockSpec((D, tn), lambda i, j: (0, j)),
            pl.BlockSpec((D, LANES), lambda i, j: (0, 0)),
            pl.BlockSpec((2 * N_HEADS, D), lambda i, j: (0, 0)),
            pl.BlockSpec((1, tn), lambda i, j: (0, 0)),
        ],
        out_specs=[
            pl.BlockSpec((tm, tn), lambda i, j: (i, j)),
            pl.BlockSpec((tm, LANES), lambda i, j: (i, 0)),
            pl.BlockSpec((2 * N_HEADS, tm), lambda i, j: (0, i)),
        ],
        out_shape=[
            jax.ShapeDtypeStruct((T, n_cols), BF16),
            jax.ShapeDtypeStruct((T, LANES), F32),
            jax.ShapeDtypeStruct((2 * N_HEADS, T), F32),
        ],
        scratch_shapes=[pltpu.VMEM((tm, D), BF16)],
        compiler_params=pltpu.CompilerParams(
            dimension_semantics=("parallel", "arbitrary"), vmem_limit_bytes=VMEM_LIMIT),
        name="norm_inproj",
    )(x2d, g, w_main, w_gate, w_gate_t, qk_gain)


def _mlstm_kernel(qk_ref, v_ref, o_ref, ga_ref, gates_ref, gates_t_ref,
                  convw_ref, bif_row_ref, bif_col_ref, ng_ref, bg_ref,
                  out_ref, xbuf_ref, c_ref, m_ref, *, L):
    c = pl.program_id(1)
    d_qk = N_HEADS * DQK_A

    @pl.when(c == 0)
    def _():
        xbuf_ref[0:CONV_HALO, :] = jnp.zeros((CONV_HALO, xbuf_ref.shape[1]), F32)
        c_ref[...] = jnp.zeros(c_ref.shape, F32)
        m_ref[...] = jnp.zeros(m_ref.shape, F32)

    xbuf_ref[CONV_HALO:CONV_HALO + L, :] = qk_ref[...].astype(F32)
    w = convw_ref[...]
    y = w[CONV_K - 1:CONV_K, :] * xbuf_ref[CONV_HALO:CONV_HALO + L, :]
    for tap in range(CONV_K - 1):
        off = CONV_HALO - (CONV_K - 1) + tap
        y = y + w[tap:tap + 1, :] * xbuf_ref[off:off + L, :]
    xbuf_ref[0:CONV_HALO, :] = xbuf_ref[L:L + CONV_HALO, :]
    qk = y * _sigmoid(y)

    row = lax.broadcasted_iota(jnp.int32, (L, L), 0)
    col = lax.broadcasted_iota(jnp.int32, (L, L), 1)
    causal = col <= row
    tri = jnp.where(causal, 1.0, 0.0).astype(F32)
    pre_col = gates_ref[...] + bif_row_ref[...]
    b_col_all = jnp.dot(tri, _log_sigmoid(pre_col), preferred_element_type=F32,
                        precision=lax.Precision.HIGHEST)
    pre_row = gates_t_ref[0] + bif_col_ref[...]
    b_row_all = lax.dot_general(_log_sigmoid(pre_row), tri, (((1,), (1,)), ((), ())),
                                preferred_element_type=F32, precision=lax.Precision.HIGHEST)

    lane = lax.broadcasted_iota(jnp.int32, (L, LANES), 1)
    heads = range(N_HEADS)
    sl = [slice(h * HEAD_DIM, (h + 1) * HEAD_DIM) for h in heads]

    head_lanes = jnp.logical_and(lane >= N_HEADS, lane < 2 * N_HEADS)
    b_all = jnp.where(head_lanes, b_col_all, 0.0)
    i_all = jnp.where(head_lanes, pltpu.roll(pre_col, N_HEADS, axis=1), 0.0)
    rowi = lax.broadcasted_iota(jnp.int32, (L, LANES), 0)
    g_max = i_all - b_all
    shift = 1
    while shift < L:
        g_max = jnp.maximum(g_max, jnp.where(rowi >= shift, pltpu.roll(g_max, shift, axis=0), -jnp.inf))
        shift *= 2
    m_prev = m_ref[...]
    m_rel = jnp.maximum(m_prev, g_max)
    inter_all = jnp.exp(m_prev - m_rel)
    floor_all = jnp.exp(-(b_all + m_rel))
    b_last = b_all[L - 1:L, :]
    a_all = b_last - b_all + i_all
    m_loc = jnp.max(a_all, axis=0, keepdims=True)
    w_all = jnp.exp(a_all - m_loc)
    m_new = jnp.maximum(b_last + m_prev, m_loc)
    sp_all = jnp.exp(b_last + m_prev - m_new)
    sc_all = jnp.exp(m_loc - m_new)
    m_ref[...] = m_new

    def head_col(x, h):
        return x[:, N_HEADS + h:N_HEADS + h + 1]


    q_pair, km_b, vh, c0, s_qk, q_c, c_loc = ({} for _ in range(7))
    ones_cols = jnp.ones((L, HEAD_DIM), BF16)
    mean_cols = jnp.full((HEAD_DIM, HEAD_DIM), 1.0 / HEAD_DIM, BF16)
    for h in heads:
        pair, half = divmod(h, 2)
        q_pair[h] = qk[:, pair * LANES:(pair + 1) * LANES].astype(BF16)
        k_pair = qk[:, d_qk + pair * LANES:d_qk + (pair + 1) * LANES]
        in_head = jnp.logical_and(lane >= half * DQK_A, lane < (half + 1) * DQK_A)
        km_b[h] = jnp.where(in_head, k_pair * (DQK_A ** -0.5), 0.0).astype(BF16)
        vh[h] = v_ref[:, sl[h]]
        c0[h] = c_ref[h]
    for h in heads:
        s_qk[h] = lax.dot_general(q_pair[h], km_b[h], (((1,), (1,)), ((), ())), preferred_element_type=F32)
    for h in heads:
        q_c[h] = jnp.dot(q_pair[h], c0[h].astype(BF16), preferred_element_type=F32)
    for h in heads:
        w_t = head_col(w_all, h)
        wv = jnp.concatenate([w_t * vh[h].astype(F32), jnp.broadcast_to(w_t, (L, HEAD_DIM))],
                             axis=1).astype(BF16)
        c_loc[h] = lax.dot_general(km_b[h], wv, (((0,), (0,)), ((), ())), preferred_element_type=F32)

    p = {}
    for h in heads:
        g_s = pre_row[h:h + 1, :] - b_row_all[N_HEADS + h:N_HEADS + h + 1, :]
        p[h] = jnp.exp(jnp.where(causal, g_s - head_col(m_rel, h), -jnp.inf)) * s_qk[h]

    p_v = {h: jnp.dot(p[h].astype(BF16), jnp.concatenate([vh[h], ones_cols], axis=1),
                      preferred_element_type=F32) for h in heads}

    for h in heads:
        both = p_v[h] + head_col(inter_all, h) * q_c[h]
        num, den = both[:, :HEAD_DIM], both[:, HEAD_DIM:]
        hh = num * (1.0 / jnp.maximum(jnp.abs(den), head_col(floor_all, h)))
        ms = jnp.dot((hh * hh).astype(BF16), mean_cols, preferred_element_type=F32)
        hn = hh * lax.rsqrt(ms + EPS) * ng_ref[:, sl[h]]
        gate = 1.0 / ((1.0 + _exp_neg(o_ref[:, sl[h]].astype(F32)))
                      * (1.0 + _exp_neg(ga_ref[:, sl[h]].astype(F32) + bg_ref[:, sl[h]])))
        out_ref[:, sl[h]] = (gate * hn).astype(out_ref.dtype)

    for h in heads:
        c_ref[h] = head_col(sp_all, h) * c0[h] + head_col(sc_all, h) * c_loc[h]


def _mlstm(p, gates, gates_t, conv_w, bif_row, bif_col, norm_g, b_gate_a, B, S, L):
    T = B * S
    D = N_HEADS * HEAD_DIM
    nc = S // L
    tok = lambda blk: pl.BlockSpec((L, D), lambda b, c: (b * nc + c, blk))
    const = lambda shape: pl.BlockSpec(shape, lambda b, c: (0,) * len(shape))
    return pl.pallas_call(
        functools.partial(_mlstm_kernel, L=L),
        grid=(B, nc),
        in_specs=[
            tok(BLK_QK_A), tok(BLK_V_A), tok(BLK_O_A), tok(BLK_G_A),
            pl.BlockSpec((L, LANES), lambda b, c: (b * nc + c, 0)),
            pl.BlockSpec((1, 2 * N_HEADS, L), lambda b, c: (b * nc + c, 0, 0)),
            const((CONV_K, D)), const((1, LANES)), const((2 * N_HEADS, 1)), const((1, D)), const((1, D)),
        ],
        out_specs=pl.BlockSpec((L, D), lambda b, c: (b * nc + c, 0)),
        out_shape=jax.ShapeDtypeStruct((T, D), BF16),
        scratch_shapes=[
            pltpu.VMEM((L + CONV_HALO, D), F32),
            pltpu.VMEM((N_HEADS, LANES, 2 * HEAD_DIM), F32),
            pltpu.VMEM((1, LANES), F32),
        ],
        compiler_params=pltpu.CompilerParams(
            dimension_semantics=("parallel", "arbitrary"), vmem_limit_bytes=VMEM_LIMIT),
        name="mlstm",
    )(p, p, p, p, gates, gates_t, conv_w, bif_row, bif_col, norm_g, b_gate_a)


def _sb_kernel(q_ref, k_ref, v_ref, gb_ref, ya_ref, bg_ref, csm_ref, out_ref, acc_ref, carry_ref, *, tq, hp):
    i = pl.program_id(2)
    csm = csm_ref[...]

    def cols(h):
        return slice(h * HEAD_DIM, (h + 1) * HEAD_DIM)

    def keys(h, j, n):
        return k_ref[pl.ds(pl.multiple_of(j * tq, tq), n * tq), cols(h)]

    def values(h, j, n):
        return v_ref[pl.ds(pl.multiple_of(j * tq, tq), n * tq), cols(h)]

    def scores_z(h, kj):
        return lax.dot_general(q_ref[:, cols(h)], kj, (((1,), (1,)), ((), ())), preferred_element_type=F32)

    def softplus2(z):
        return jnp.maximum(z, 0.0) + jnp.log2(1.0 + jnp.exp2(-jnp.abs(z)))

    def scores(h, kj):
        z = scores_z(h, kj)
        return z, softplus2(z)

    def suffix(nlk):
        return jnp.dot(nlk.astype(BF16), csm, preferred_element_type=F32)

    def total(nlk):
        return -jnp.sum(nlk, axis=1, keepdims=True)

    def strict_mask():
        row = lax.broadcasted_iota(jnp.int32, (tq, tq), 0)
        col = lax.broadcasted_iota(jnp.int32, (tq, tq), 1)
        return col < row

    @pl.when(i == 0)
    def _():
        strict = strict_mask()
        for h in range(hp):
            z, nlk = scores(h, keys(h, 0, 1))
            nlk = jnp.where(strict, nlk, 0.0)
            a = jnp.where(strict, jnp.exp2(z - nlk + suffix(nlk)), 0.0)
            acc_ref[h] = jnp.dot(a.astype(BF16), values(h, 0, 1), preferred_element_type=F32)

    @pl.when(i > 0)
    def _():
        strict = strict_mask()
        alive = [None] * hp
        for group in range(0, hp, SB_STAGE_HEADS):
            heads = range(group, min(group + SB_STAGE_HEADS, hp))
            z_all = {h: scores_z(h, keys(h, i - 1, 2)) for h in heads}
            d_all, nlk_d, nlk_p = {}, {}, {}
            for h in heads:
                nlk = softplus2(z_all[h])
                d_all[h] = z_all[h] - nlk
                nlk_d[h] = jnp.where(strict, nlk[:, tq:], 0.0)
                nlk_p[h] = nlk[:, :tq]
            cs_d = {h: suffix(nlk_d[h]) for h in heads}
            cs_p = {h: suffix(nlk_p[h]) for h in heads}
            carry_d = {h: total(nlk_d[h]) for h in heads}
            a_all = {}
            for h in heads:
                a_d = jnp.where(strict, jnp.exp2(d_all[h][:, tq:] + cs_d[h]), 0.0)
                a_p = jnp.exp2(d_all[h][:, :tq] + cs_p[h] + carry_d[h])
                a_all[h] = jnp.concatenate([a_p, a_d], axis=1).astype(BF16)
            pv = {h: jnp.dot(a_all[h], values(h, i - 1, 2), preferred_element_type=F32) for h in heads}
            for h in heads:
                acc_ref[h] = pv[h]
                carry = carry_d[h] + total(nlk_p[h])
                carry_ref[h] = jnp.broadcast_to(carry, (tq, LANES))
                alive[h] = (jnp.max(carry) > UNDERFLOW_LOG2).astype(jnp.int32)

        def cond(state):
            j, alive_h = state
            return jnp.logical_and(j >= 0, alive_h > 0)

        for h in range(hp):
            def body(state, h=h):
                j, _ = state
                z, nlk = scores(h, keys(h, j, 1))
                carry = carry_ref[h][:, 0:1]
                a = jnp.exp2(z - nlk + suffix(nlk) + carry)
                acc_ref[h] += jnp.dot(a.astype(BF16), values(h, j, 1), preferred_element_type=F32)
                carry = carry + total(nlk)
                carry_ref[h] = jnp.broadcast_to(carry, (tq, LANES))
                return j - 1, (jnp.max(carry) > UNDERFLOW_LOG2).astype(jnp.int32)

            lax.while_loop(cond, body, (i - 2, alive[h]))

    for h in range(hp):
        gate = _sigmoid(gb_ref[:, cols(h)].astype(F32) + bg_ref[:, cols(h)])
        out_ref[:, cols(h)] = (ya_ref[:, cols(h)].astype(F32) + gate * acc_ref[h]).astype(out_ref.dtype)


def _stick_breaking(p, ya, b_gate_b, csm, B, S, tq, hp):
    T = B * S
    D = N_HEADS * HEAD_DIM
    nq = S // tq
    w = hp * HEAD_DIM
    groups = N_HEADS // hp
    tile = lambda blk: pl.BlockSpec((tq, w), lambda b, g, i: (b * nq + i, blk * groups + g))
    seq = lambda blk: pl.BlockSpec((S, w), lambda b, g, i: (b, blk * groups + g))
    return pl.pallas_call(
        functools.partial(_sb_kernel, tq=tq, hp=hp),
        grid=(B, groups, nq),
        in_specs=[
            tile(BLK_Q_B), seq(BLK_K_B), seq(BLK_V_B), tile(BLK_G_B),
            pl.BlockSpec((tq, w), lambda b, g, i: (b * nq + i, g)),
            pl.BlockSpec((1, w), lambda b, g, i: (0, g)),
            pl.BlockSpec((tq, tq), lambda b, g, i: (0, 0)),
        ],
        out_specs=pl.BlockSpec((tq, w), lambda b, g, i: (b * nq + i, g)),
        out_shape=jax.ShapeDtypeStruct((T, D), BF16),
        scratch_shapes=[pltpu.VMEM((hp, tq, HEAD_DIM), F32), pltpu.VMEM((hp, tq, LANES), F32)],
        compiler_params=pltpu.CompilerParams(
            dimension_semantics=("parallel", "parallel", "arbitrary"), vmem_limit_bytes=VMEM_LIMIT),
        name="stick_breaking",
    )(p, p, p, p, ya, b_gate_b, csm)


def _mlp_kernel(x_ref, y_ref, wo_ref, g_ref, wu_ref, wd_ref, out_ref, h_ref):
    f = pl.program_id(1)

    @pl.when(f == 0)
    def _():
        x1 = x_ref[...] + jnp.dot(y_ref[...], wo_ref[...], preferred_element_type=F32)
        out_ref[...] = x1
        ms = jnp.mean(x1 * x1, axis=-1, keepdims=True)
        h_ref[...] = (x1 * lax.rsqrt(ms + EPS) * g_ref[...]).astype(BF16)

    u = jnp.maximum(jnp.dot(h_ref[...], wu_ref[...], preferred_element_type=F32), 0.0)
    out_ref[...] += jnp.dot((u * u).astype(BF16), wd_ref[...], preferred_element_type=F32)


def _outproj_mlp(x2d, y, w_out, g, w_up, w_down, tm, tf):
    T, D = x2d.shape
    F = w_up.shape[1]
    return pl.pallas_call(
        _mlp_kernel,
        grid=(T // tm, F // tf),
        in_specs=[
            pl.BlockSpec((tm, D), lambda i, f: (i, 0)),
            pl.BlockSpec((tm, D), lambda i, f: (i, 0)),
            pl.BlockSpec((D, D), lambda i, f: (0, 0)),
            pl.BlockSpec((1, D), lambda i, f: (0, 0)),
            pl.BlockSpec((D, tf), lambda i, f: (0, f)),
            pl.BlockSpec((tf, D), lambda i, f: (f, 0)),
        ],
        out_specs=pl.BlockSpec((tm, D), lambda i, f: (i, 0)),
        out_shape=jax.ShapeDtypeStruct((T, D), F32),
        scratch_shapes=[pltpu.VMEM((tm, D), BF16)],
        compiler_params=pltpu.CompilerParams(
            dimension_semantics=("parallel", "arbitrary"), vmem_limit_bytes=VMEM_LIMIT),
        name="outproj_mlp",
    )(x2d, y, w_out, g, w_up, w_down)


def _tiles(B, S):
    T = B * S
    return dict(
        tm_in=min(1024, T),
        chunk=min(256, S),
        tq=min(256, S),
        sb_heads=4,
        tm_mlp=min(1024, T),
        tf=1024,
    )


def kernel(x, norm_mix_g, w_in, b_if, b_gate, conv_w, mlstm_norm_g, sb_q_norm_g, sb_k_norm_g,
           w_out, norm_mlp_g, w_up, w_down):
    B, S, D = x.shape
    depth = w_in.shape[0]
    T = B * S
    assert D == N_HEADS * HEAD_DIM and w_in.shape[2] == 8 * D + 2 * N_HEADS
    cfg = _tiles(B, S)
    tq = cfg["tq"]
    gate_lo = 3 * D
    gate_hi = gate_lo + 2 * N_HEADS

    r = lax.broadcasted_iota(jnp.int32, (tq, tq), 0)
    c = lax.broadcasted_iota(jnp.int32, (tq, tq), 1)
    csm = jnp.where(r > c, -1.0, 0.0).astype(BF16)

    x2d = x.reshape(T, D)
    for l in range(depth):
        w_nogate = jnp.concatenate([w_in[l][:, :gate_lo], w_in[l][:, gate_hi:]], axis=1)
        src = {blk: pos for pos, blk in enumerate(W_IN_ORDER)}
        w_main = jnp.concatenate([w_nogate[:, src[b] * D:(src[b] + 1) * D] for b in range(N_BLK)],
                                 axis=1).astype(BF16)
        w_gate = jnp.pad(w_in[l][:, gate_lo:gate_hi], ((0, 0), (0, LANES - 2 * N_HEADS))).astype(BF16)
        w_gate_t = w_in[l][:, gate_lo:gate_hi].T.astype(BF16)
        qk_gain = jnp.concatenate([jnp.tile(sb_q_norm_g[l], N_HEADS),
                                   jnp.tile(sb_k_norm_g[l] * (HEAD_DIM ** -0.5 * LOG2_E), N_HEADS)]
                                  ).reshape(1, 2 * D)
        p, gates, gates_t = _inproj(x2d, norm_mix_g[l].reshape(1, D), w_main, w_gate, w_gate_t,
                                    qk_gain, cfg["tm_in"])
        bif_row = jnp.pad(b_if[l], (0, LANES - 2 * N_HEADS)).reshape(1, LANES)
        bif_col = b_if[l].reshape(2 * N_HEADS, 1)
        L = cfg["chunk"]
        gates_t = gates_t.reshape(2 * N_HEADS, T // L, L).transpose(1, 0, 2)
        ya = _mlstm(p, gates, gates_t, conv_w[l], bif_row, bif_col, mlstm_norm_g[l].reshape(1, D),
                    b_gate[l][:D].reshape(1, D), B, S, cfg["chunk"])
        y = _stick_breaking(p, ya, b_gate[l][D:].reshape(1, D), csm, B, S, tq, cfg["sb_heads"])
        x2d = _outproj_mlp(x2d, y, w_out[l].astype(BF16), norm_mlp_g[l].reshape(1, D),
                           w_up[l].astype(BF16), w_down[l].astype(BF16), cfg["tm_mlp"], cfg["tf"])
    return x2d.reshape(B, S, D)
```

```python
import functools

import jax
import jax.numpy as jnp
from jax import lax
from jax.experimental import pallas as pl
from jax.experimental.pallas import tpu as pltpu

F32 = jnp.float32
BF16 = jnp.bfloat16

EPS = 1e-6
N_HEADS = 8
HEAD_DIM = 128
DQK_A = 64
CONV_K = 4
LANES = 128
CONV_HALO = 8
VMEM_LIMIT = 56 * 1024 * 1024
LOG2_E = 1.4426950408889634
UNDERFLOW_LOG2 = -110.0 * LOG2_E

W_IN_BLOCKS = (0, 1, 2, 6, 7, 5, 3, 4)
N_BLK = 8
INPROJ_TN = 2
PW_QK_A, PW_V_A, PW_GATE_B, PW_V_B, PW_Q_B, PW_K_B = range(6)


def _log_sigmoid(x):
    return jnp.minimum(x, 0.0) - jnp.log(1.0 + jnp.exp2(jnp.abs(x) * -LOG2_E))


def _exp_neg(x):
    return jnp.exp2(x * -LOG2_E)


def _sigmoid(x):
    return 1.0 / (1.0 + _exp_neg(x))


def _wprep_kernel(w_ref, main_ref, gate_ref):
    D = main_ref.shape[2] // N_BLK
    n_gate = 2 * N_HEADS
    gate_lo = 3 * D
    for b, pos in enumerate(W_IN_BLOCKS):
        lo = pos * D + (n_gate if pos >= 3 else 0)
        main_ref[0, :, b * D:(b + 1) * D] = w_ref[0, :, lo:lo + D].astype(main_ref.dtype)
    g = w_ref[0, :, gate_lo:gate_lo + LANES]
    lane = lax.broadcasted_iota(jnp.int32, g.shape, 1)
    gate_ref[0] = jnp.where(lane < n_gate, g, 0.0).astype(gate_ref.dtype)


def _wprep(w_in, rows):
    depth, D, n_in = w_in.shape
    return pl.pallas_call(
        _wprep_kernel,
        grid=(depth, D // rows),
        in_specs=[pl.BlockSpec((1, rows, n_in), lambda l, r: (l, r, 0))],
        out_specs=[pl.BlockSpec((1, rows, N_BLK * D), lambda l, r: (l, r, 0)),
                   pl.BlockSpec((1, rows, LANES), lambda l, r: (l, r, 0))],
        out_shape=[jax.ShapeDtypeStruct((depth, D, N_BLK * D), BF16),
                   jax.ShapeDtypeStruct((depth, D, LANES), BF16)],
        compiler_params=pltpu.CompilerParams(
            dimension_semantics=("parallel", "parallel"), vmem_limit_bytes=VMEM_LIMIT),
        name="inproj_weights",
    )(w_in)


def _inproj_kernel(x_ref, g_ref, w_ref, wg_ref, qkg_ref, bga_ref, bgb_ref,
                   p_ref, ga_ref, gates_ref, gates_t_ref, h_ref):
    j = pl.program_id(1)
    D = x_ref.shape[1]

    @pl.when(j == 0)
    def _():
        x = x_ref[...]
        ms = jnp.mean(x * x, axis=-1, keepdims=True)
        h = (x * lax.rsqrt(ms + EPS) * g_ref[...]).astype(BF16)
        h_ref[...] = h
        gates = jnp.dot(h, wg_ref[0], preferred_element_type=F32)
        gates_ref[...] = gates
        gates_t_ref[...] = gates.T[:2 * N_HEADS, :]
        acc = jnp.dot(h, w_ref[0], preferred_element_type=F32)
        p_ref[...] = acc.astype(p_ref.dtype)

    @pl.when(j == 1)
    def _():
        acc = jnp.dot(h_ref[...], w_ref[0], preferred_element_type=F32)
        gate = 1.0 / ((1.0 + _exp_neg(acc[:, :D])) * (1.0 + _exp_neg(acc[:, D:] + bga_ref[...])))
        ga_ref[...] = gate.astype(ga_ref.dtype)

    @pl.when(j == 2)
    def _():
        acc = jnp.dot(h_ref[...], w_ref[0], preferred_element_type=F32)
        p_ref[:, :D] = _sigmoid(acc[:, :D] + bgb_ref[...]).astype(p_ref.dtype)
        p_ref[:, D:] = acc[:, D:].astype(p_ref.dtype)

    @pl.when(j == 3)
    def _():
        acc = jnp.dot(h_ref[...], w_ref[0], preferred_element_type=F32)
        gain = qkg_ref[...]
        for h in range(2 * N_HEADS):
            sl = slice(h * HEAD_DIM, (h + 1) * HEAD_DIM)
            a = acc[:, sl]
            ms = jnp.mean(a * a, axis=-1, keepdims=True)
            p_ref[:, sl] = (a * lax.rsqrt(ms + EPS) * gain[:, sl]).astype(p_ref.dtype)


def _inproj(x2d, g, w_main, w_gate, layer, qk_gain, b_gate_a, b_gate_b, tm):
    T, D = x2d.shape
    tn = INPROJ_TN * D
    n_tiles = w_main.shape[2] // tn
    assert n_tiles == 4 and qk_gain.shape == (1, tn)
    const = lambda shape: pl.BlockSpec(shape, lambda i, j: (0,) * len(shape))
    return pl.pallas_call(
        _inproj_kernel,
        grid=(T // tm, n_tiles),
        in_specs=[
            pl.BlockSpec((tm, D), lambda i, j: (i, 0)),
            const((1, D)),
            pl.BlockSpec((1, D, tn), lambda i, j: (layer, 0, j)),
            pl.BlockSpec((1, D, LANES), lambda i, j: (layer, 0, 0)),
            const((1, tn)), const((1, D)), const((1, D)),
        ],
        out_specs=[
            pl.BlockSpec((tm, tn), lambda i, j: (i, jnp.maximum(j - 1, 0))),
            pl.BlockSpec((tm, D), lambda i, j: (i, 0)),
            pl.BlockSpec((tm, LANES), lambda i, j: (i, 0)),
            pl.BlockSpec((2 * N_HEADS, tm), lambda i, j: (0, i)),
        ],
        out_shape=[
            jax.ShapeDtypeStruct((T, (n_tiles - 1) * tn), BF16),
            jax.ShapeDtypeStruct((T, D), BF16),
            jax.ShapeDtypeStruct((T, LANES), F32),
            jax.ShapeDtypeStruct((2 * N_HEADS, T), F32),
        ],
        scratch_shapes=[pltpu.VMEM((tm, D), BF16)],
        compiler_params=pltpu.CompilerParams(
            dimension_semantics=("parallel", "arbitrary"), vmem_limit_bytes=VMEM_LIMIT),
        name="norm_inproj",
    )(x2d, g, w_main, w_gate, qk_gain, b_gate_a, b_gate_b)


def _mixers_kernel(qk_ref, va_ref, ga_ref, gates_ref, gates_t_ref, q_ref, k_ref, v_ref, gb_ref,
                   convw_ref, bif_row_ref, bif_col_ref, ng_ref, csm_ref,
                   out_ref, xbuf_ref, c_ref, m_ref, ya_ref, acc_ref, carry_ref, *, L):
    c = pl.program_id(1)
    d_qk = N_HEADS * DQK_A
    heads = range(N_HEADS)
    sl = [slice(h * HEAD_DIM, (h + 1) * HEAD_DIM) for h in heads]
    tq = L
    csm = csm_ref[...]

    def keys(h, j, n):
        return k_ref[pl.ds(pl.multiple_of(j * tq, tq), n * tq), sl[h]]

    def values(h, j, n):
        return v_ref[pl.ds(pl.multiple_of(j * tq, tq), n * tq), sl[h]]

    def scores_z(h, kj):
        return lax.dot_general(q_ref[:, sl[h]], kj, (((1,), (1,)), ((), ())), preferred_element_type=F32)

    def softplus2(z):
        return jnp.maximum(z, 0.0) + jnp.log2(1.0 + jnp.exp2(-jnp.abs(z)))

    def suffix(nlk):
        return jnp.dot(nlk.astype(BF16), csm, preferred_element_type=F32)

    def total(nlk):
        return -jnp.sum(nlk, axis=1, keepdims=True)

    def strict_mask():
        row = lax.broadcasted_iota(jnp.int32, (tq, tq), 0)
        col = lax.broadcasted_iota(jnp.int32, (tq, tq), 1)
        return col < row

    def head_col(x, h):
        return x[:, N_HEADS + h:N_HEADS + h + 1]

    def mlstm_prepare(st):
        xbuf_ref[CONV_HALO:CONV_HALO + L, :] = qk_ref[...].astype(F32)
        w = convw_ref[...]
        y = w[CONV_K - 1:CONV_K, :] * xbuf_ref[CONV_HALO:CONV_HALO + L, :]
        for tap in range(CONV_K - 1):
            off = CONV_HALO - (CONV_K - 1) + tap
            y = y + w[tap:tap + 1, :] * xbuf_ref[off:off + L, :]
        xbuf_ref[0:CONV_HALO, :] = xbuf_ref[L:L + CONV_HALO, :]
        qk = y * _sigmoid(y)

        row = lax.broadcasted_iota(jnp.int32, (L, L), 0)
        col = lax.broadcasted_iota(jnp.int32, (L, L), 1)
        st["causal"] = causal = col <= row
        tri = jnp.where(causal, 1.0, 0.0).astype(F32)
        pre_col = gates_ref[...] + bif_row_ref[...]
        b_col_all = jnp.dot(tri, _log_sigmoid(pre_col), preferred_element_type=F32,
                            precision=lax.Precision.HIGHEST)
        st["pre_row"] = pre_row = gates_t_ref[0] + bif_col_ref[...]
        st["b_row_all"] = lax.dot_general(_log_sigmoid(pre_row), tri, (((1,), (1,)), ((), ())),
                                          preferred_element_type=F32, precision=lax.Precision.HIGHEST)

        lane = lax.broadcasted_iota(jnp.int32, (L, LANES), 1)
        head_lanes = jnp.logical_and(lane >= N_HEADS, lane < 2 * N_HEADS)
        b_all = jnp.where(head_lanes, b_col_all, 0.0)
        i_all = jnp.where(head_lanes, pltpu.roll(pre_col, N_HEADS, axis=1), 0.0)
        rowi = lax.broadcasted_iota(jnp.int32, (L, LANES), 0)
        g_max = i_all - b_all
        shift = 1
        while shift < L:
            g_max = jnp.maximum(g_max, jnp.where(rowi >= shift, pltpu.roll(g_max, shift, axis=0), -jnp.inf))
            shift *= 2
        m_prev = m_ref[...]
        st["m_rel"] = m_rel = jnp.maximum(m_prev, g_max)
        st["inter_all"] = jnp.exp(m_prev - m_rel)
        st["floor_all"] = jnp.exp(-(b_all + m_rel))
        b_last = b_all[L - 1:L, :]
        a_all = b_last - b_all + i_all
        m_loc = jnp.max(a_all, axis=0, keepdims=True)
        w_all = jnp.exp(a_all - m_loc)
        m_new = jnp.maximum(b_last + m_prev, m_loc)
        st["sp_all"] = jnp.exp(b_last + m_prev - m_new)
        st["sc_all"] = jnp.exp(m_loc - m_new)
        m_ref[...] = m_new

        q_pair, km_b, vh, c0, s_qk, q_c, c_loc = ({} for _ in range(7))
        for h in heads:
            pair, half = divmod(h, 2)
            q_pair[h] = qk[:, pair * LANES:(pair + 1) * LANES].astype(BF16)
            k_pair = qk[:, d_qk + pair * LANES:d_qk + (pair + 1) * LANES]
            in_head = jnp.logical_and(lane >= half * DQK_A, lane < (half + 1) * DQK_A)
            km_b[h] = jnp.where(in_head, k_pair * (DQK_A ** -0.5), 0.0).astype(BF16)
            vh[h] = va_ref[:, sl[h]]
            c0[h] = c_ref[h]
        for h in heads:
            s_qk[h] = lax.dot_general(q_pair[h], km_b[h], (((1,), (1,)), ((), ())), preferred_element_type=F32)
        for h in heads:
            q_c[h] = jnp.dot(q_pair[h], c0[h].astype(BF16), preferred_element_type=F32)
        for h in heads:
            w_t = head_col(w_all, h)
            wv = jnp.concatenate([w_t * vh[h].astype(F32), jnp.broadcast_to(w_t, (L, HEAD_DIM))],
                                 axis=1).astype(BF16)
            c_loc[h] = lax.dot_general(km_b[h], wv, (((0,), (0,)), ((), ())), preferred_element_type=F32)
        st.update(vh=vh, c0=c0, s_qk=s_qk, q_c=q_c, c_loc=c_loc)

    def mlstm_weights(st):
        p = {}
        for h in heads:
            g_s = st["pre_row"][h:h + 1, :] - st["b_row_all"][N_HEADS + h:N_HEADS + h + 1, :]
            p[h] = jnp.exp(jnp.where(st["causal"], g_s - head_col(st["m_rel"], h), -jnp.inf)) * st["s_qk"][h]
        st["p"] = p

    def mlstm_numerators(st):
        ones_cols = jnp.ones((L, HEAD_DIM), BF16)
        st["p_v"] = {h: jnp.dot(st["p"][h].astype(BF16), jnp.concatenate([st["vh"][h], ones_cols], axis=1),
                                preferred_element_type=F32) for h in heads}

    def mlstm_outputs(st):
        mean_cols = jnp.full((HEAD_DIM, HEAD_DIM), 1.0 / HEAD_DIM, BF16)
        for h in heads:
            both = st["p_v"][h] + head_col(st["inter_all"], h) * st["q_c"][h]
            num, den = both[:, :HEAD_DIM], both[:, HEAD_DIM:]
            hh = num * (1.0 / jnp.maximum(jnp.abs(den), head_col(st["floor_all"], h)))
            ms = jnp.dot((hh * hh).astype(BF16), mean_cols, preferred_element_type=F32)
            hn = hh * lax.rsqrt(ms + EPS) * ng_ref[:, sl[h]]
            ya_ref[:, sl[h]] = ga_ref[:, sl[h]].astype(F32) * hn
        for h in heads:
            c_ref[h] = head_col(st["sp_all"], h) * st["c0"][h] + head_col(st["sc_all"], h) * st["c_loc"][h]

    @pl.when(c == 0)
    def _():
        xbuf_ref[0:CONV_HALO, :] = jnp.zeros((CONV_HALO, xbuf_ref.shape[1]), F32)
        c_ref[...] = jnp.zeros(c_ref.shape, F32)
        m_ref[...] = jnp.zeros(m_ref.shape, F32)
        st = {}
        mlstm_prepare(st)
        strict = strict_mask()
        z_all = {h: scores_z(h, keys(h, 0, 1)) for h in heads}
        mlstm_weights(st)
        d_all, nlk_all = {}, {}
        for h in heads:
            nlk = softplus2(z_all[h])
            d_all[h] = z_all[h] - nlk
            nlk_all[h] = jnp.where(strict, nlk, 0.0)
        cs = {h: suffix(nlk_all[h]) for h in heads}
        mlstm_numerators(st)
        a_all = {h: jnp.where(strict, jnp.exp2(d_all[h] + cs[h]), 0.0).astype(BF16) for h in heads}
        for h in heads:
            acc_ref[h] = jnp.dot(a_all[h], values(h, 0, 1), preferred_element_type=F32)
        mlstm_outputs(st)

    @pl.when(c > 0)
    def _():
        st = {}
        mlstm_prepare(st)
        strict = strict_mask()
        alive = [None] * N_HEADS
        z_all = {h: scores_z(h, keys(h, c - 1, 2)) for h in heads}
        mlstm_weights(st)
        d_all, nlk_d, nlk_p = {}, {}, {}
        for h in heads:
            nlk = softplus2(z_all[h])
            d_all[h] = z_all[h] - nlk
            nlk_d[h] = jnp.where(strict, nlk[:, tq:], 0.0)
            nlk_p[h] = nlk[:, :tq]
        cs_d = {h: suffix(nlk_d[h]) for h in heads}
        cs_p = {h: suffix(nlk_p[h]) for h in heads}
        mlstm_numerators(st)
        carry_d = {h: total(nlk_d[h]) for h in heads}
        a_all = {}
        for h in heads:
            a_d = jnp.where(strict, jnp.exp2(d_all[h][:, tq:] + cs_d[h]), 0.0)
            a_p = jnp.exp2(d_all[h][:, :tq] + cs_p[h] + carry_d[h])
            a_all[h] = jnp.concatenate([a_p, a_d], axis=1).astype(BF16)
        pv = {h: jnp.dot(a_all[h], values(h, c - 1, 2), preferred_element_type=F32) for h in heads}
        mlstm_outputs(st)
        for h in heads:
            acc_ref[h] = pv[h]
            carry = carry_d[h] + total(nlk_p[h])
            carry_ref[h] = jnp.broadcast_to(carry, (tq, LANES))
            alive[h] = (jnp.max(carry) > UNDERFLOW_LOG2).astype(jnp.int32)

        def cond(state):
            j, alive_h = state
            return jnp.logical_and(j >= 0, alive_h > 0)

        for h in heads:
            def body(state, h=h):
                j, _ = state
                z = scores_z(h, keys(h, j, 1))
                nlk = softplus2(z)
                carry = carry_ref[h][:, 0:1]
                a = jnp.exp2(z - nlk + suffix(nlk) + carry)
                acc_ref[h] += jnp.dot(a.astype(BF16), values(h, j, 1), preferred_element_type=F32)
                carry = carry + total(nlk)
                carry_ref[h] = jnp.broadcast_to(carry, (tq, LANES))
                return j - 1, (jnp.max(carry) > UNDERFLOW_LOG2).astype(jnp.int32)

            lax.while_loop(cond, body, (c - 2, alive[h]))

    for h in heads:
        out_ref[:, sl[h]] = (ya_ref[:, sl[h]] + gb_ref[:, sl[h]].astype(F32) * acc_ref[h]).astype(out_ref.dtype)


def _mixers(p, gate_a, gates, gates_t, conv_w, bif_row, bif_col, norm_g, csm, B, S, L):
    T = B * S
    D = N_HEADS * HEAD_DIM
    nc = S // L
    tok = lambda blk: pl.BlockSpec((L, D), lambda b, c: (b * nc + c, blk))
    seq = lambda blk: pl.BlockSpec((S, D), lambda b, c: (b, blk), pipeline_mode=pl.Buffered(1))
    const = lambda shape: pl.BlockSpec(shape, lambda b, c: (0,) * len(shape))
    return pl.pallas_call(
        functools.partial(_mixers_kernel, L=L),
        grid=(B, nc),
        in_specs=[
            tok(PW_QK_A), tok(PW_V_A), tok(0),
            pl.BlockSpec((L, LANES), lambda b, c: (b * nc + c, 0)),
            pl.BlockSpec((1, 2 * N_HEADS, L), lambda b, c: (b * nc + c, 0, 0)),
            tok(PW_Q_B), seq(PW_K_B), seq(PW_V_B), tok(PW_GATE_B),
            const((CONV_K, D)), const((1, LANES)), const((2 * N_HEADS, 1)), const((1, D)),
            const((L, L)),
        ],
        out_specs=pl.BlockSpec((L, D), lambda b, c: (b * nc + c, 0)),
        out_shape=jax.ShapeDtypeStruct((T, D), BF16),
        scratch_shapes=[
            pltpu.VMEM((L + CONV_HALO, D), F32),
            pltpu.VMEM((N_HEADS, LANES, 2 * HEAD_DIM), F32),
            pltpu.VMEM((1, LANES), F32),
            pltpu.VMEM((L, D), F32),
            pltpu.VMEM((N_HEADS, L, HEAD_DIM), F32),
            pltpu.VMEM((N_HEADS, L, LANES), F32),
        ],
        compiler_params=pltpu.CompilerParams(
            dimension_semantics=("parallel", "arbitrary"), vmem_limit_bytes=VMEM_LIMIT),
        name="mixers",
    )(p, p, gate_a, gates, gates_t, p, p, p, p, conv_w, bif_row, bif_col, norm_g, csm)


def _mlp_kernel(x_ref, y_ref, wo_ref, g_ref, wu_ref, wd_ref, out_ref, h_ref):
    f = pl.program_id(1)

    @pl.when(f == 0)
    def _():
        x1 = x_ref[...] + jnp.dot(y_ref[...], wo_ref[...], preferred_element_type=F32)
        out_ref[...] = x1
        ms = jnp.mean(x1 * x1, axis=-1, keepdims=True)
        h_ref[...] = (x1 * lax.rsqrt(ms + EPS) * g_ref[...]).astype(BF16)

    u = jnp.maximum(jnp.dot(h_ref[...], wu_ref[...], preferred_element_type=F32), 0.0)
    out_ref[...] += jnp.dot((u * u).astype(BF16), wd_ref[...], preferred_element_type=F32)


def _outproj_mlp(x2d, y, w_out, g, w_up, w_down, tm, tf):
    T, D = x2d.shape
    F = w_up.shape[1]
    return pl.pallas_call(
        _mlp_kernel,
        grid=(T // tm, F // tf),
        in_specs=[
            pl.BlockSpec((tm, D), lambda i, f: (i, 0)),
            pl.BlockSpec((tm, D), lambda i, f: (i, 0)),
            pl.BlockSpec((D, D), lambda i, f: (0, 0)),
            pl.BlockSpec((1, D), lambda i, f: (0, 0)),
            pl.BlockSpec((D, tf), lambda i, f: (0, f)),
            pl.BlockSpec((tf, D), lambda i, f: (f, 0)),
        ],
        out_specs=pl.BlockSpec((tm, D), lambda i, f: (i, 0)),
        out_shape=jax.ShapeDtypeStruct((T, D), F32),
        scratch_shapes=[pltpu.VMEM((tm, D), BF16)],
        compiler_params=pltpu.CompilerParams(
            dimension_semantics=("parallel", "arbitrary"), vmem_limit_bytes=VMEM_LIMIT),
        name="outproj_mlp",
    )(x2d, y, w_out, g, w_up, w_down)


def _tiles(B, S):
    T = B * S
    return dict(
        wprep_rows=128,
        tm_in=min(1024, T),
        chunk=min(256, S),
        tm_mlp=min(1024, T),
        tf=1024,
    )


def kernel(x, norm_mix_g, w_in, b_if, b_gate, conv_w, mlstm_norm_g, sb_q_norm_g, sb_k_norm_g,
           w_out, norm_mlp_g, w_up, w_down):
    B, S, D = x.shape
    depth = w_in.shape[0]
    T = B * S
    assert D == N_HEADS * HEAD_DIM and w_in.shape[2] == 8 * D + 2 * N_HEADS
    cfg = _tiles(B, S)
    tq = cfg["chunk"]

    r = lax.broadcasted_iota(jnp.int32, (tq, tq), 0)
    c = lax.broadcasted_iota(jnp.int32, (tq, tq), 1)
    csm = jnp.where(r > c, -1.0, 0.0).astype(BF16)

    w_main_all, w_gate_all = _wprep(w_in, cfg["wprep_rows"])

    x2d = x.reshape(T, D)
    for l in range(depth):
        qk_gain = jnp.concatenate([jnp.tile(sb_q_norm_g[l], N_HEADS),
                                   jnp.tile(sb_k_norm_g[l] * (HEAD_DIM ** -0.5 * LOG2_E), N_HEADS)]
                                  ).reshape(1, 2 * D)
        p, gate_a, gates, gates_t = _inproj(
            x2d, norm_mix_g[l].reshape(1, D), w_main_all, w_gate_all, l, qk_gain,
            b_gate[l][:D].reshape(1, D), b_gate[l][D:].reshape(1, D), cfg["tm_in"])
        bif_row = jnp.pad(b_if[l], (0, LANES - 2 * N_HEADS)).reshape(1, LANES)
        bif_col = b_if[l].reshape(2 * N_HEADS, 1)
        L = cfg["chunk"]
        gates_t = gates_t.reshape(2 * N_HEADS, T // L, L).transpose(1, 0, 2)
        y = _mixers(p, gate_a, gates, gates_t, conv_w[l], bif_row, bif_col, mlstm_norm_g[l].reshape(1, D),
                    csm, B, S, L)
        x2d = _outproj_mlp(x2d, y, w_out[l].astype(BF16), norm_mlp_g[l].reshape(1, D),
                           w_up[l].astype(BF16), w_down[l].astype(BF16), cfg["tm_mlp"], cfg["tf"])
    return x2d.reshape(B, S, D)
```

```python
import functools

import jax
import jax.numpy as jnp
from jax import lax
from jax.experimental import pallas as pl
from jax.experimental.pallas import tpu as pltpu

F32 = jnp.float32
BF16 = jnp.bfloat16

EPS = 1e-6
N_HEADS = 8
HEAD_DIM = 128
DQK_A = 64
CONV_K = 4
LANES = 128
CONV_HALO = 8
VMEM_LIMIT = 56 * 1024 * 1024
LOG2_E = 1.4426950408889634
UNDERFLOW_LOG2 = -110.0 * LOG2_E

W_IN_BLOCKS = (0, 1, 2, 6, 7, 5, 3, 4)
N_BLK = 8
INPROJ_TN = 2
PW_QK_A, PW_V_A, PW_GATE_B, PW_V_B, PW_Q_B, PW_K_B = range(6)


def _log_sigmoid(x):
    return jnp.minimum(x, 0.0) - jnp.log(1.0 + jnp.exp2(jnp.abs(x) * -LOG2_E))


def _exp_neg(x):
    return jnp.exp2(x * -LOG2_E)


def _sigmoid(x):
    return 1.0 / (1.0 + _exp_neg(x))


def _wprep_kernel(w_ref, wg_ref, main_ref, gate_ref):
    main_ref[0] = w_ref[0].astype(main_ref.dtype)
    g = wg_ref[0]
    row = lax.broadcasted_iota(jnp.int32, g.shape, 0)
    gate_ref[0] = jnp.where(row < 2 * N_HEADS, g, 0.0).astype(gate_ref.dtype)


def _wprep(w_in_t):
    depth, n_in, D = w_in_t.shape
    n_gate = 2 * N_HEADS
    gate_lo = 3 * D

    def block_row(b):
        pos = functools.reduce(lambda acc, kv: jnp.where(b == kv[0], kv[1], acc), enumerate(W_IN_BLOCKS), 0)
        return pl.multiple_of(pos * D + jnp.where(pos >= 3, n_gate, 0), n_gate)

    return pl.pallas_call(
        _wprep_kernel,
        grid=(depth, N_BLK),
        in_specs=[pl.BlockSpec((pl.Element(1), pl.Element(D), pl.Element(D)), lambda l, b: (l, block_row(b), 0)),
                  pl.BlockSpec((pl.Element(1), pl.Element(LANES), pl.Element(D)), lambda l, b: (l, gate_lo, 0))],
        out_specs=[pl.BlockSpec((1, D, D), lambda l, b: (l, b, 0)),
                   pl.BlockSpec((1, LANES, D), lambda l, b: (l, 0, 0))],
        out_shape=[jax.ShapeDtypeStruct((depth, N_BLK * D, D), BF16),
                   jax.ShapeDtypeStruct((depth, LANES, D), BF16)],
        compiler_params=pltpu.CompilerParams(
            dimension_semantics=("parallel", "arbitrary"), vmem_limit_bytes=VMEM_LIMIT),
        name="inproj_weights",
    )(w_in_t, w_in_t)


def _dot_nt(a, b_t):
    return lax.dot_general(a, b_t, (((1,), (1,)), ((), ())), preferred_element_type=F32)


def _inproj_kernel(x_ref, g_ref, w_ref, wg_ref, qkg_ref, bga_ref, bgb_ref,
                   p_ref, ga_ref, gates_ref, gates_t_ref, h_ref):
    j = pl.program_id(1)
    D = x_ref.shape[1]

    @pl.when(j == 0)
    def _():
        x = x_ref[...]
        ms = jnp.mean(x * x, axis=-1, keepdims=True)
        h = (x * lax.rsqrt(ms + EPS) * g_ref[...]).astype(BF16)
        h_ref[...] = h
        gates = _dot_nt(h, wg_ref[0])
        gates_ref[...] = gates
        gates_t_ref[...] = gates.T[:2 * N_HEADS, :]
        acc = _dot_nt(h, w_ref[0])
        p_ref[...] = acc.astype(p_ref.dtype)

    @pl.when(j == 1)
    def _():
        acc = _dot_nt(h_ref[...], w_ref[0])
        gate = 1.0 / ((1.0 + _exp_neg(acc[:, :D])) * (1.0 + _exp_neg(acc[:, D:] + bga_ref[...])))
        ga_ref[...] = gate.astype(ga_ref.dtype)

    @pl.when(j == 2)
    def _():
        acc = _dot_nt(h_ref[...], w_ref[0])
        p_ref[:, :D] = _sigmoid(acc[:, :D] + bgb_ref[...]).astype(p_ref.dtype)
        p_ref[:, D:] = acc[:, D:].astype(p_ref.dtype)

    @pl.when(j == 3)
    def _():
        acc = _dot_nt(h_ref[...], w_ref[0])
        gain = qkg_ref[...]
        for h in range(2 * N_HEADS):
            sl = slice(h * HEAD_DIM, (h + 1) * HEAD_DIM)
            a = acc[:, sl]
            ms = jnp.mean(a * a, axis=-1, keepdims=True)
            p_ref[:, sl] = (a * lax.rsqrt(ms + EPS) * gain[:, sl]).astype(p_ref.dtype)


def _inproj(x2d, g, w_main, w_gate, layer, qk_gain, b_gate_a, b_gate_b, tm):
    T, D = x2d.shape
    tn = INPROJ_TN * D
    n_tiles = w_main.shape[1] // tn
    assert n_tiles == 4 and qk_gain.shape == (1, tn)
    const = lambda shape: pl.BlockSpec(shape, lambda i, j: (0,) * len(shape))
    return pl.pallas_call(
        _inproj_kernel,
        grid=(T // tm, n_tiles),
        in_specs=[
            pl.BlockSpec((tm, D), lambda i, j: (i, 0)),
            const((1, D)),
            pl.BlockSpec((1, tn, D), lambda i, j: (layer, j, 0)),
            pl.BlockSpec((1, LANES, D), lambda i, j: (layer, 0, 0)),
            const((1, tn)), const((1, D)), const((1, D)),
        ],
        out_specs=[
            pl.BlockSpec((tm, tn), lambda i, j: (i, jnp.maximum(j - 1, 0))),
            pl.BlockSpec((tm, D), lambda i, j: (i, 0)),
            pl.BlockSpec((tm, LANES), lambda i, j: (i, 0)),
            pl.BlockSpec((2 * N_HEADS, tm), lambda i, j: (0, i)),
        ],
        out_shape=[
            jax.ShapeDtypeStruct((T, (n_tiles - 1) * tn), BF16),
            jax.ShapeDtypeStruct((T, D), BF16),
            jax.ShapeDtypeStruct((T, LANES), F32),
            jax.ShapeDtypeStruct((2 * N_HEADS, T), F32),
        ],
        scratch_shapes=[pltpu.VMEM((tm, D), BF16)],
        compiler_params=pltpu.CompilerParams(
            dimension_semantics=("parallel", "arbitrary"), vmem_limit_bytes=VMEM_LIMIT),
        name="norm_inproj",
    )(x2d, g, w_main, w_gate, qk_gain, b_gate_a, b_gate_b)


def _mixers_kernel(qk_ref, va_ref, ga_ref, gates_ref, gates_t_ref, q_ref, k_ref, v_ref, gb_ref,
                   convw_ref, bif_row_ref, bif_col_ref, ng_ref, csm_ref,
                   out_ref, xbuf_ref, c_ref, m_ref, ya_ref, acc_ref, carry_ref, *, L):
    c = pl.program_id(1)
    d_qk = N_HEADS * DQK_A
    heads = range(N_HEADS)
    sl = [slice(h * HEAD_DIM, (h + 1) * HEAD_DIM) for h in heads]
    tq = L
    csm = csm_ref[...]

    def keys(h, j, n):
        return k_ref[pl.ds(pl.multiple_of(j * tq, tq), n * tq), sl[h]]

    def values(h, j, n):
        return v_ref[pl.ds(pl.multiple_of(j * tq, tq), n * tq), sl[h]]

    def scores_z(h, kj):
        return lax.dot_general(q_ref[:, sl[h]], kj, (((1,), (1,)), ((), ())), preferred_element_type=F32)

    def softplus2(z):
        return jnp.maximum(z, 0.0) + jnp.log2(1.0 + jnp.exp2(-jnp.abs(z)))

    def suffix(nlk):
        return jnp.dot(nlk.astype(BF16), csm, preferred_element_type=F32)

    def total(nlk):
        return -jnp.sum(nlk, axis=1, keepdims=True)

    def strict_mask():
        row = lax.broadcasted_iota(jnp.int32, (tq, tq), 0)
        col = lax.broadcasted_iota(jnp.int32, (tq, tq), 1)
        return col < row

    def head_col(x, h):
        return x[:, N_HEADS + h:N_HEADS + h + 1]

    def mlstm_prepare(st):
        xbuf_ref[CONV_HALO:CONV_HALO + L, :] = qk_ref[...].astype(F32)
        w = convw_ref[...]
        y = w[CONV_K - 1:CONV_K, :] * xbuf_ref[CONV_HALO:CONV_HALO + L, :]
        for tap in range(CONV_K - 1):
            off = CONV_HALO - (CONV_K - 1) + tap
            y = y + w[tap:tap + 1, :] * xbuf_ref[off:off + L, :]
        xbuf_ref[0:CONV_HALO, :] = xbuf_ref[L:L + CONV_HALO, :]
        qk = y * _sigmoid(y)

        row = lax.broadcasted_iota(jnp.int32, (L, L), 0)
        col = lax.broadcasted_iota(jnp.int32, (L, L), 1)
        st["causal"] = causal = col <= row
        tri = jnp.where(causal, 1.0, 0.0).astype(F32)
        pre_col = gates_ref[...] + bif_row_ref[...]
        b_col_all = jnp.dot(tri, _log_sigmoid(pre_col), preferred_element_type=F32,
                            precision=lax.Precision.HIGHEST)
        st["pre_row"] = pre_row = gates_t_ref[0] + bif_col_ref[...]
        st["b_row_all"] = lax.dot_general(_log_sigmoid(pre_row), tri, (((1,), (1,)), ((), ())),
                                          preferred_element_type=F32, precision=lax.Precision.HIGHEST)

        lane = lax.broadcasted_iota(jnp.int32, (L, LANES), 1)
        head_lanes = jnp.logical_and(lane >= N_HEADS, lane < 2 * N_HEADS)
        b_all = jnp.where(head_lanes, b_col_all, 0.0)
        i_all = jnp.where(head_lanes, pltpu.roll(pre_col, N_HEADS, axis=1), 0.0)
        rowi = lax.broadcasted_iota(jnp.int32, (L, LANES), 0)
        g_max = i_all - b_all
        shift = 1
        while shift < L:
            g_max = jnp.maximum(g_max, jnp.where(rowi >= shift, pltpu.roll(g_max, shift, axis=0), -jnp.inf))
            shift *= 2
        m_prev = m_ref[...]
        st["m_rel"] = m_rel = jnp.maximum(m_prev, g_max)
        st["inter_all"] = jnp.exp(m_prev - m_rel)
        st["floor_all"] = jnp.exp(-(b_all + m_rel))
        b_last = b_all[L - 1:L, :]
        a_all = b_last - b_all + i_all
        m_loc = jnp.max(a_all, axis=0, keepdims=True)
        w_all = jnp.exp(a_all - m_loc)
        m_new = jnp.maximum(b_last + m_prev, m_loc)
        st["sp_all"] = jnp.exp(b_last + m_prev - m_new)
        st["sc_all"] = jnp.exp(m_loc - m_new)
        m_ref[...] = m_new

        q_pair, km_b, vh, c0, s_qk, q_c, c_loc = ({} for _ in range(7))
        for h in heads:
            pair, half = divmod(h, 2)
            q_pair[h] = qk[:, pair * LANES:(pair + 1) * LANES].astype(BF16)
            k_pair = qk[:, d_qk + pair * LANES:d_qk + (pair + 1) * LANES]
            in_head = jnp.logical_and(lane >= half * DQK_A, lane < (half + 1) * DQK_A)
            km_b[h] = jnp.where(in_head, k_pair * (DQK_A ** -0.5), 0.0).astype(BF16)
            vh[h] = va_ref[:, sl[h]]
            c0[h] = c_ref[h]
        for h in heads:
            s_qk[h] = lax.dot_general(q_pair[h], km_b[h], (((1,), (1,)), ((), ())), preferred_element_type=F32)
        for h in heads:
            q_c[h] = jnp.dot(q_pair[h], c0[h].astype(BF16), preferred_element_type=F32)
        for h in heads:
            w_t = head_col(w_all, h)
            wv = jnp.concatenate([w_t * vh[h].astype(F32), jnp.broadcast_to(w_t, (L, HEAD_DIM))],
                                 axis=1).astype(BF16)
            c_loc[h] = lax.dot_general(km_b[h], wv, (((0,), (0,)), ((), ())), preferred_element_type=F32)
        st.update(vh=vh, c0=c0, s_qk=s_qk, q_c=q_c, c_loc=c_loc)

    def mlstm_weights(st):
        p = {}
        for h in heads:
            g_s = st["pre_row"][h:h + 1, :] - st["b_row_all"][N_HEADS + h:N_HEADS + h + 1, :]
            p[h] = jnp.exp(jnp.where(st["causal"], g_s - head_col(st["m_rel"], h), -jnp.inf)) * st["s_qk"][h]
        st["p"] = p

    def mlstm_numerators(st):
        ones_cols = jnp.ones((L, HEAD_DIM), BF16)
        st["p_v"] = {h: jnp.dot(st["p"][h].astype(BF16), jnp.concatenate([st["vh"][h], ones_cols], axis=1),
                                preferred_element_type=F32) for h in heads}

    def mlstm_outputs(st):
        mean_cols = jnp.full((HEAD_DIM, HEAD_DIM), 1.0 / HEAD_DIM, BF16)
        for h in heads:
            both = st["p_v"][h] + head_col(st["inter_all"], h) * st["q_c"][h]
            num, den = both[:, :HEAD_DIM], both[:, HEAD_DIM:]
            hh = num * (1.0 / jnp.maximum(jnp.abs(den), head_col(st["floor_all"], h)))
            ms = jnp.dot((hh * hh).astype(BF16), mean_cols, preferred_element_type=F32)
            hn = hh * lax.rsqrt(ms + EPS) * ng_ref[:, sl[h]]
            ya_ref[:, sl[h]] = ga_ref[:, sl[h]].astype(F32) * hn
        for h in heads:
            c_ref[h] = head_col(st["sp_all"], h) * st["c0"][h] + head_col(st["sc_all"], h) * st["c_loc"][h]

    @pl.when(c == 0)
    def _():
        xbuf_ref[0:CONV_HALO, :] = jnp.zeros((CONV_HALO, xbuf_ref.shape[1]), F32)
        c_ref[...] = jnp.zeros(c_ref.shape, F32)
        m_ref[...] = jnp.zeros(m_ref.shape, F32)
        st = {}
        mlstm_prepare(st)
        strict = strict_mask()
        z_all = {h: scores_z(h, keys(h, 0, 1)) for h in heads}
        mlstm_weights(st)
        d_all, nlk_all = {}, {}
        for h in heads:
            nlk = softplus2(z_all[h])
            d_all[h] = z_all[h] - nlk
            nlk_all[h] = jnp.where(strict, nlk, 0.0)
        cs = {h: suffix(nlk_all[h]) for h in heads}
        mlstm_numerators(st)
        a_all = {h: jnp.where(strict, jnp.exp2(d_all[h] + cs[h]), 0.0).astype(BF16) for h in heads}
        for h in heads:
            acc_ref[h] = jnp.dot(a_all[h], values(h, 0, 1), preferred_element_type=F32)
        mlstm_outputs(st)

    @pl.when(c > 0)
    def _():
        st = {}
        mlstm_prepare(st)
        strict = strict_mask()
        alive = [None] * N_HEADS
        z_all = {h: scores_z(h, keys(h, c - 1, 2)) for h in heads}
        mlstm_weights(st)
        d_all, nlk_d, nlk_p = {}, {}, {}
        for h in heads:
            nlk = softplus2(z_all[h])
            d_all[h] = z_all[h] - nlk
            nlk_d[h] = jnp.where(strict, nlk[:, tq:], 0.0)
            nlk_p[h] = nlk[:, :tq]
        cs_d = {h: suffix(nlk_d[h]) for h in heads}
        cs_p = {h: suffix(nlk_p[h]) for h in heads}
        mlstm_numerators(st)
        carry_d = {h: total(nlk_d[h]) for h in heads}
        a_all = {}
        for h in heads:
            a_d = jnp.where(strict, jnp.exp2(d_all[h][:, tq:] + cs_d[h]), 0.0)
            a_p = jnp.exp2(d_all[h][:, :tq] + cs_p[h] + carry_d[h])
            a_all[h] = jnp.concatenate([a_p, a_d], axis=1).astype(BF16)
        pv = {h: jnp.dot(a_all[h], values(h, c - 1, 2), preferred_element_type=F32) for h in heads}
        mlstm_outputs(st)
        for h in heads:
            acc_ref[h] = pv[h]
            carry = carry_d[h] + total(nlk_p[h])
            carry_ref[h] = jnp.broadcast_to(carry, (tq, LANES))
            alive[h] = (jnp.max(carry) > UNDERFLOW_LOG2).astype(jnp.int32)

        def cond(state):
            j, alive_h = state
            return jnp.logical_and(j >= 0, alive_h > 0)

        for h in heads:
            def body(state, h=h):
                j, _ = state
                z = scores_z(h, keys(h, j, 1))
                nlk = softplus2(z)
                carry = carry_ref[h][:, 0:1]
                a = jnp.exp2(z - nlk + suffix(nlk) + carry)
                acc_ref[h] += jnp.dot(a.astype(BF16), values(h, j, 1), preferred_element_type=F32)
                carry = carry + total(nlk)
                carry_ref[h] = jnp.broadcast_to(carry, (tq, LANES))
                return j - 1, (jnp.max(carry) > UNDERFLOW_LOG2).astype(jnp.int32)

            lax.while_loop(cond, body, (c - 2, alive[h]))

    for h in heads:
        out_ref[:, sl[h]] = (ya_ref[:, sl[h]] + gb_ref[:, sl[h]].astype(F32) * acc_ref[h]).astype(out_ref.dtype)


def _mixers(p, gate_a, gates, gates_t, conv_w, bif_row, bif_col, norm_g, csm, B, S, L):
    T = B * S
    D = N_HEADS * HEAD_DIM
    nc = S // L
    tok = lambda blk: pl.BlockSpec((L, D), lambda b, c: (b * nc + c, blk))
    seq = lambda blk: pl.BlockSpec((S, D), lambda b, c: (b, blk), pipeline_mode=pl.Buffered(1))
    const = lambda shape: pl.BlockSpec(shape, lambda b, c: (0,) * len(shape))
    return pl.pallas_call(
        functools.partial(_mixers_kernel, L=L),
        grid=(B, nc),
        in_specs=[
            tok(PW_QK_A), tok(PW_V_A), tok(0),
            pl.BlockSpec((L, LANES), lambda b, c: (b * nc + c, 0)),
            pl.BlockSpec((1, 2 * N_HEADS, L), lambda b, c: (b * nc + c, 0, 0)),
            tok(PW_Q_B), seq(PW_K_B), seq(PW_V_B), tok(PW_GATE_B),
            const((CONV_K, D)), const((1, LANES)), const((2 * N_HEADS, 1)), const((1, D)),
            const((L, L)),
        ],
        out_specs=pl.BlockSpec((L, D), lambda b, c: (b * nc + c, 0)),
        out_shape=jax.ShapeDtypeStruct((T, D), BF16),
        scratch_shapes=[
            pltpu.VMEM((L + CONV_HALO, D), F32),
            pltpu.VMEM((N_HEADS, LANES, 2 * HEAD_DIM), F32),
            pltpu.VMEM((1, LANES), F32),
            pltpu.VMEM((L, D), F32),
            pltpu.VMEM((N_HEADS, L, HEAD_DIM), F32),
            pltpu.VMEM((N_HEADS, L, LANES), F32),
        ],
        compiler_params=pltpu.CompilerParams(
            dimension_semantics=("parallel", "arbitrary"), vmem_limit_bytes=VMEM_LIMIT),
        name="mixers",
    )(p, p, gate_a, gates, gates_t, p, p, p, p, conv_w, bif_row, bif_col, norm_g, csm)


def _mlp_kernel(x_ref, y_ref, wo_ref, g_ref, wu_ref, wd_ref, out_ref, h_ref):
    f = pl.program_id(1)

    @pl.when(f == 0)
    def _():
        x1 = x_ref[...] + jnp.dot(y_ref[...], wo_ref[...], preferred_element_type=F32)
        out_ref[...] = x1
        ms = jnp.mean(x1 * x1, axis=-1, keepdims=True)
        h_ref[...] = (x1 * lax.rsqrt(ms + EPS) * g_ref[...]).astype(BF16)

    u = jnp.maximum(jnp.dot(h_ref[...], wu_ref[...], preferred_element_type=F32), 0.0)
    out_ref[...] += jnp.dot((u * u).astype(BF16), wd_ref[...], preferred_element_type=F32)


def _outproj_mlp(x2d, y, w_out, g, w_up, w_down, tm, tf):
    T, D = x2d.shape
    F = w_up.shape[1]
    return pl.pallas_call(
        _mlp_kernel,
        grid=(T // tm, F // tf),
        in_specs=[
            pl.BlockSpec((tm, D), lambda i, f: (i, 0)),
            pl.BlockSpec((tm, D), lambda i, f: (i, 0)),
            pl.BlockSpec((D, D), lambda i, f: (0, 0)),
            pl.BlockSpec((1, D), lambda i, f: (0, 0)),
            pl.BlockSpec((D, tf), lambda i, f: (0, f)),
            pl.BlockSpec((tf, D), lambda i, f: (f, 0)),
        ],
        out_specs=pl.BlockSpec((tm, D), lambda i, f: (i, 0)),
        out_shape=jax.ShapeDtypeStruct((T, D), F32),
        scratch_shapes=[pltpu.VMEM((tm, D), BF16)],
        compiler_params=pltpu.CompilerParams(
            dimension_semantics=("parallel", "arbitrary"), vmem_limit_bytes=VMEM_LIMIT),
        name="outproj_mlp",
    )(x2d, y, w_out, g, w_up, w_down)


def _tiles(B, S):
    T = B * S
    return dict(
        tm_in=min(1024, T),
        chunk=min(256, S),
        tm_mlp=min(1024, T),
        tf=1024,
    )


def kernel(x, norm_mix_g, w_in, b_if, b_gate, conv_w, mlstm_norm_g, sb_q_norm_g, sb_k_norm_g,
           w_out, norm_mlp_g, w_up, w_down):
    B, S, D = x.shape
    depth = w_in.shape[0]
    T = B * S
    assert D == N_HEADS * HEAD_DIM and w_in.shape[2] == 8 * D + 2 * N_HEADS
    cfg = _tiles(B, S)
    tq = cfg["chunk"]

    r = lax.broadcasted_iota(jnp.int32, (tq, tq), 0)
    c = lax.broadcasted_iota(jnp.int32, (tq, tq), 1)
    csm = jnp.where(r > c, -1.0, 0.0).astype(BF16)

    w_main_all, w_gate_all = _wprep(jnp.swapaxes(w_in, 1, 2))

    x2d = x.reshape(T, D)
    for l in range(depth):
        qk_gain = jnp.concatenate([jnp.tile(sb_q_norm_g[l], N_HEADS),
                                   jnp.tile(sb_k_norm_g[l] * (HEAD_DIM ** -0.5 * LOG2_E), N_HEADS)]
                                  ).reshape(1, 2 * D)
        p, gate_a, gates, gates_t = _inproj(
            x2d, norm_mix_g[l].reshape(1, D), w_main_all, w_gate_all, l, qk_gain,
            b_gate[l][:D].reshape(1, D), b_gate[l][D:].reshape(1, D), cfg["tm_in"])
        bif_row = jnp.pad(b_if[l], (0, LANES - 2 * N_HEADS)).reshape(1, LANES)
        bif_col = b_if[l].reshape(2 * N_HEADS, 1)
        L = cfg["chunk"]
        gates_t = gates_t.reshape(2 * N_HEADS, T // L, L).transpose(1, 0, 2)
        y = _mixers(p, gate_a, gates, gates_t, conv_w[l], bif_row, bif_col, mlstm_norm_g[l].reshape(1, D),
                    csm, B, S, L)
        x2d = _outproj_mlp(x2d, y, w_out[l].astype(BF16), norm_mlp_g[l].reshape(1, D),
                           w_up[l].astype(BF16), w_down[l].astype(BF16), cfg["tm_mlp"], cfg["tf"])
    return x2d.reshape(B, S, D)
```

```python
import functools

import jax
import jax.numpy as jnp
from jax import lax
from jax.experimental import pallas as pl
from jax.experimental.pallas import tpu as pltpu

F32 = jnp.float32
BF16 = jnp.bfloat16

EPS = 1e-6
N_HEADS = 8
HEAD_DIM = 128
DQK_A = 64
CONV_K = 4
LANES = 128
CONV_HALO = 8
VMEM_LIMIT = 56 * 1024 * 1024
LOG2_E = 1.4426950408889634
UNDERFLOW_LOG2 = -110.0 * LOG2_E

W_IN_BLOCKS = (0, 1, 2, 6, 7, 5, 3, 4)
N_BLK = 8
INPROJ_TN = 2
SB_GROUP = 2
PW_QK_A, PW_V_A, PW_GATE_B, PW_V_B, PW_Q_B, PW_K_B = range(6)


def _log_sigmoid(x):
    return jnp.minimum(x, 0.0) - jnp.log(1.0 + jnp.exp2(jnp.abs(x) * -LOG2_E))


def _exp_neg(x):
    return jnp.exp2(x * -LOG2_E)


def _sigmoid(x):
    return 1.0 / (1.0 + _exp_neg(x))


def _wprep_kernel(w_ref, wg_ref, main_ref, gate_ref):
    main_ref[0] = w_ref[0].astype(main_ref.dtype)
    g = wg_ref[0]
    row = lax.broadcasted_iota(jnp.int32, g.shape, 0)
    gate_ref[0] = jnp.where(row < 2 * N_HEADS, g, 0.0).astype(gate_ref.dtype)


def _wprep(w_in_t):
    depth, n_in, D = w_in_t.shape
    n_gate = 2 * N_HEADS
    gate_lo = 3 * D

    def block_row(b):
        pos = functools.reduce(lambda acc, kv: jnp.where(b == kv[0], kv[1], acc), enumerate(W_IN_BLOCKS), 0)
        return pl.multiple_of(pos * D + jnp.where(pos >= 3, n_gate, 0), n_gate)

    return pl.pallas_call(
        _wprep_kernel,
        grid=(depth, N_BLK),
        in_specs=[pl.BlockSpec((pl.Element(1), pl.Element(D), pl.Element(D)), lambda l, b: (l, block_row(b), 0)),
                  pl.BlockSpec((pl.Element(1), pl.Element(LANES), pl.Element(D)), lambda l, b: (l, gate_lo, 0))],
        out_specs=[pl.BlockSpec((1, D, D), lambda l, b: (l, b, 0)),
                   pl.BlockSpec((1, LANES, D), lambda l, b: (l, 0, 0))],
        out_shape=[jax.ShapeDtypeStruct((depth, N_BLK * D, D), BF16),
                   jax.ShapeDtypeStruct((depth, LANES, D), BF16)],
        compiler_params=pltpu.CompilerParams(
            dimension_semantics=("parallel", "arbitrary"), vmem_limit_bytes=VMEM_LIMIT),
        name="inproj_weights",
    )(w_in_t, w_in_t)


def _dot_nt(a, b_t):
    return lax.dot_general(a, b_t, (((1,), (1,)), ((), ())), preferred_element_type=F32)


def _inproj_kernel(x_ref, g_ref, w_ref, wg_ref, qkg_ref, bga_ref, bgb_ref,
                   p_ref, ga_ref, gates_ref, gates_t_ref, h_ref):
    j = pl.program_id(1)
    D = x_ref.shape[1]

    @pl.when(j == 0)
    def _():
        x = x_ref[...]
        ms = jnp.mean(x * x, axis=-1, keepdims=True)
        h = (x * lax.rsqrt(ms + EPS) * g_ref[...]).astype(BF16)
        h_ref[...] = h
        gates = _dot_nt(h, wg_ref[0])
        gates_ref[...] = gates
        gates_t_ref[...] = gates.T[:2 * N_HEADS, :]
        acc = _dot_nt(h, w_ref[0])
        p_ref[...] = acc.astype(p_ref.dtype)

    @pl.when(j == 1)
    def _():
        acc = _dot_nt(h_ref[...], w_ref[0])
        gate = 1.0 / ((1.0 + _exp_neg(acc[:, :D])) * (1.0 + _exp_neg(acc[:, D:] + bga_ref[...])))
        ga_ref[...] = gate.astype(ga_ref.dtype)

    @pl.when(j == 2)
    def _():
        acc = _dot_nt(h_ref[...], w_ref[0])
        p_ref[:, :D] = _sigmoid(acc[:, :D] + bgb_ref[...]).astype(p_ref.dtype)
        p_ref[:, D:] = acc[:, D:].astype(p_ref.dtype)

    @pl.when(j == 3)
    def _():
        acc = _dot_nt(h_ref[...], w_ref[0])
        gain = qkg_ref[...]
        for h in range(2 * N_HEADS):
            sl = slice(h * HEAD_DIM, (h + 1) * HEAD_DIM)
            a = acc[:, sl]
            ms = jnp.mean(a * a, axis=-1, keepdims=True)
            p_ref[:, sl] = (a * lax.rsqrt(ms + EPS) * gain[:, sl]).astype(p_ref.dtype)


def _inproj(x2d, g, w_main, w_gate, layer, qk_gain, b_gate_a, b_gate_b, tm):
    T, D = x2d.shape
    tn = INPROJ_TN * D
    n_tiles = w_main.shape[1] // tn
    assert n_tiles == 4 and qk_gain.shape == (1, tn)
    const = lambda shape: pl.BlockSpec(shape, lambda i, j: (0,) * len(shape))
    return pl.pallas_call(
        _inproj_kernel,
        grid=(T // tm, n_tiles),
        in_specs=[
            pl.BlockSpec((tm, D), lambda i, j: (i, 0)),
            const((1, D)),
            pl.BlockSpec((1, tn, D), lambda i, j: (layer, j, 0)),
            pl.BlockSpec((1, LANES, D), lambda i, j: (layer, 0, 0)),
            const((1, tn)), const((1, D)), const((1, D)),
        ],
        out_specs=[
            pl.BlockSpec((tm, tn), lambda i, j: (i, jnp.maximum(j - 1, 0))),
            pl.BlockSpec((tm, D), lambda i, j: (i, 0)),
            pl.BlockSpec((tm, LANES), lambda i, j: (i, 0)),
            pl.BlockSpec((2 * N_HEADS, tm), lambda i, j: (0, i)),
        ],
        out_shape=[
            jax.ShapeDtypeStruct((T, (n_tiles - 1) * tn), BF16),
            jax.ShapeDtypeStruct((T, D), BF16),
            jax.ShapeDtypeStruct((T, LANES), F32),
            jax.ShapeDtypeStruct((2 * N_HEADS, T), F32),
        ],
        scratch_shapes=[pltpu.VMEM((tm, D), BF16)],
        compiler_params=pltpu.CompilerParams(
            dimension_semantics=("parallel", "arbitrary"), vmem_limit_bytes=VMEM_LIMIT),
        name="norm_inproj",
    )(x2d, g, w_main, w_gate, qk_gain, b_gate_a, b_gate_b)


def _mixers_kernel(qk_ref, va_ref, ga_ref, gates_ref, gates_t_ref, q_ref, k_ref, v_ref, gb_ref,
                   convw_ref, bif_row_ref, bif_col_ref, ng_ref, csm_ref,
                   out_ref, xbuf_ref, c_ref, m_ref, ya_ref, acc_ref, carry_ref, *, L):
    c = pl.program_id(1)
    d_qk = N_HEADS * DQK_A
    heads = range(N_HEADS)
    sl = [slice(h * HEAD_DIM, (h + 1) * HEAD_DIM) for h in heads]
    tq = L
    csm = csm_ref[...]

    def keys(h, j, n):
        return k_ref[pl.ds(pl.multiple_of(j * tq, tq), n * tq), sl[h]]

    def values(h, j, n):
        return v_ref[pl.ds(pl.multiple_of(j * tq, tq), n * tq), sl[h]]

    def scores_z(h, kj):
        return lax.dot_general(q_ref[:, sl[h]], kj, (((1,), (1,)), ((), ())), preferred_element_type=F32)

    def softplus2(z):
        return jnp.maximum(z, 0.0) + jnp.log2(1.0 + jnp.exp2(-jnp.abs(z)))

    def suffix(nlk):
        return jnp.dot(nlk.astype(BF16), csm, preferred_element_type=F32)

    def total(nlk):
        return -jnp.sum(nlk, axis=1, keepdims=True)

    def strict_mask():
        row = lax.broadcasted_iota(jnp.int32, (tq, tq), 0)
        col = lax.broadcasted_iota(jnp.int32, (tq, tq), 1)
        return col < row

    def head_col(x, h):
        return x[:, N_HEADS + h:N_HEADS + h + 1]

    def mlstm_prepare(st):
        xbuf_ref[CONV_HALO:CONV_HALO + L, :] = qk_ref[...].astype(F32)
        w = convw_ref[...]
        y = w[CONV_K - 1:CONV_K, :] * xbuf_ref[CONV_HALO:CONV_HALO + L, :]
        for tap in range(CONV_K - 1):
            off = CONV_HALO - (CONV_K - 1) + tap
            y = y + w[tap:tap + 1, :] * xbuf_ref[off:off + L, :]
        xbuf_ref[0:CONV_HALO, :] = xbuf_ref[L:L + CONV_HALO, :]
        qk = y * _sigmoid(y)

        row = lax.broadcasted_iota(jnp.int32, (L, L), 0)
        col = lax.broadcasted_iota(jnp.int32, (L, L), 1)
        st["causal"] = causal = col <= row
        tri = jnp.where(causal, 1.0, 0.0).astype(F32)
        pre_col = gates_ref[...] + bif_row_ref[...]
        b_col_all = jnp.dot(tri, _log_sigmoid(pre_col), preferred_element_type=F32,
                            precision=lax.Precision.HIGHEST)
        st["pre_row"] = pre_row = gates_t_ref[0] + bif_col_ref[...]
        st["b_row_all"] = lax.dot_general(_log_sigmoid(pre_row), tri, (((1,), (1,)), ((), ())),
                                          preferred_element_type=F32, precision=lax.Precision.HIGHEST)

        lane = lax.broadcasted_iota(jnp.int32, (L, LANES), 1)
        head_lanes = jnp.logical_and(lane >= N_HEADS, lane < 2 * N_HEADS)
        b_all = jnp.where(head_lanes, b_col_all, 0.0)
        i_all = jnp.where(head_lanes, pltpu.roll(pre_col, N_HEADS, axis=1), 0.0)
        rowi = lax.broadcasted_iota(jnp.int32, (L, LANES), 0)
        g_max = i_all - b_all
        shift = 1
        while shift < L:
            g_max = jnp.maximum(g_max, jnp.where(rowi >= shift, pltpu.roll(g_max, shift, axis=0), -jnp.inf))
            shift *= 2
        m_prev = m_ref[...]
        st["m_rel"] = m_rel = jnp.maximum(m_prev, g_max)
        st["inter_all"] = jnp.exp(m_prev - m_rel)
        st["floor_all"] = jnp.exp(-(b_all + m_rel))
        b_last = b_all[L - 1:L, :]
        a_all = b_last - b_all + i_all
        m_loc = jnp.max(a_all, axis=0, keepdims=True)
        w_all = jnp.exp(a_all - m_loc)
        m_new = jnp.maximum(b_last + m_prev, m_loc)
        st["sp_all"] = jnp.exp(b_last + m_prev - m_new)
        st["sc_all"] = jnp.exp(m_loc - m_new)
        m_ref[...] = m_new

        q_pair, km_b, vh, c0, s_qk, q_c, c_loc = ({} for _ in range(7))
        for h in heads:
            pair, half = divmod(h, 2)
            q_pair[h] = qk[:, pair * LANES:(pair + 1) * LANES].astype(BF16)
            k_pair = qk[:, d_qk + pair * LANES:d_qk + (pair + 1) * LANES]
            in_head = jnp.logical_and(lane >= half * DQK_A, lane < (half + 1) * DQK_A)
            km_b[h] = jnp.where(in_head, k_pair * (DQK_A ** -0.5), 0.0).astype(BF16)
            vh[h] = va_ref[:, sl[h]]
            c0[h] = c_ref[h]
        for h in heads:
            s_qk[h] = lax.dot_general(q_pair[h], km_b[h], (((1,), (1,)), ((), ())), preferred_element_type=F32)
        for h in heads:
            q_c[h] = jnp.dot(q_pair[h], c0[h].astype(BF16), preferred_element_type=F32)
        for h in heads:
            w_t = head_col(w_all, h)
            wv = jnp.concatenate([w_t * vh[h].astype(F32), jnp.broadcast_to(w_t, (L, HEAD_DIM))],
                                 axis=1).astype(BF16)
            c_loc[h] = lax.dot_general(km_b[h], wv, (((0,), (0,)), ((), ())), preferred_element_type=F32)
        st.update(vh=vh, c0=c0, s_qk=s_qk, q_c=q_c, c_loc=c_loc)

    def mlstm_weights(st):
        p = {}
        for h in heads:
            g_s = st["pre_row"][h:h + 1, :] - st["b_row_all"][N_HEADS + h:N_HEADS + h + 1, :]
            p[h] = jnp.exp(jnp.where(st["causal"], g_s - head_col(st["m_rel"], h), -jnp.inf)) * st["s_qk"][h]
        st["p"] = p

    def mlstm_numerators(st):
        ones_cols = jnp.ones((L, HEAD_DIM), BF16)
        st["p_v"] = {h: jnp.dot(st["p"][h].astype(BF16), jnp.concatenate([st["vh"][h], ones_cols], axis=1),
                                preferred_element_type=F32) for h in heads}

    def mlstm_outputs(st):
        mean_cols = jnp.full((HEAD_DIM, HEAD_DIM), 1.0 / HEAD_DIM, BF16)
        for h in heads:
            both = st["p_v"][h] + head_col(st["inter_all"], h) * st["q_c"][h]
            num, den = both[:, :HEAD_DIM], both[:, HEAD_DIM:]
            hh = num * (1.0 / jnp.maximum(jnp.abs(den), head_col(st["floor_all"], h)))
            ms = jnp.dot((hh * hh).astype(BF16), mean_cols, preferred_element_type=F32)
            hn = hh * lax.rsqrt(ms + EPS) * ng_ref[:, sl[h]]
            ya_ref[:, sl[h]] = ga_ref[:, sl[h]].astype(F32) * hn
        for h in heads:
            c_ref[h] = head_col(st["sp_all"], h) * st["c0"][h] + head_col(st["sc_all"], h) * st["c_loc"][h]

    @pl.when(c == 0)
    def _():
        xbuf_ref[0:CONV_HALO, :] = jnp.zeros((CONV_HALO, xbuf_ref.shape[1]), F32)
        c_ref[...] = jnp.zeros(c_ref.shape, F32)
        m_ref[...] = jnp.zeros(m_ref.shape, F32)
        st = {}
        mlstm_prepare(st)
        strict = strict_mask()
        z_all = {h: scores_z(h, keys(h, 0, 1)) for h in heads}
        mlstm_weights(st)
        d_all, nlk_all = {}, {}
        for h in heads:
            nlk = softplus2(z_all[h])
            d_all[h] = z_all[h] - nlk
            nlk_all[h] = jnp.where(strict, nlk, 0.0)
        cs = {h: suffix(nlk_all[h]) for h in heads}
        mlstm_numerators(st)
        a_all = {h: jnp.where(strict, jnp.exp2(d_all[h] + cs[h]), 0.0).astype(BF16) for h in heads}
        for h in heads:
            acc_ref[h] = jnp.dot(a_all[h], values(h, 0, 1), preferred_element_type=F32)
        mlstm_outputs(st)

    @pl.when(c > 0)
    def _():
        st = {}
        mlstm_prepare(st)
        strict = strict_mask()
        alive = [None] * N_HEADS
        def sb_scores(hs, sb):
            sb["z"] = {h: scores_z(h, keys(h, c - 1, 2)) for h in hs}

        def sb_suffix(hs, sb):
            sb["d"], sb["nd"], sb["np"] = {}, {}, {}
            for h in hs:
                nlk = softplus2(sb["z"][h])
                sb["d"][h] = sb["z"][h] - nlk
                sb["nd"][h] = jnp.where(strict, nlk[:, tq:], 0.0)
                sb["np"][h] = nlk[:, :tq]
            sb["cd"] = {h: suffix(sb["nd"][h]) for h in hs}
            sb["cp"] = {h: suffix(sb["np"][h]) for h in hs}

        def sb_values(hs, sb):
            carry_d = {h: total(sb["nd"][h]) for h in hs}
            a_all = {}
            for h in hs:
                a_d = jnp.where(strict, jnp.exp2(sb["d"][h][:, tq:] + sb["cd"][h]), 0.0)
                a_p = jnp.exp2(sb["d"][h][:, :tq] + sb["cp"][h] + carry_d[h])
                a_all[h] = jnp.concatenate([a_p, a_d], axis=1).astype(BF16)
            pv = {h: jnp.dot(a_all[h], values(h, c - 1, 2), preferred_element_type=F32) for h in hs}
            for h in hs:
                acc_ref[h] = pv[h]
                carry = carry_d[h] + total(sb["np"][h])
                carry_ref[h] = jnp.broadcast_to(carry, (tq, LANES))
                alive[h] = (jnp.max(carry) > UNDERFLOW_LOG2).astype(jnp.int32)

        groups = [range(g, g + SB_GROUP) for g in range(0, N_HEADS, SB_GROUP)]
        assert len(groups) == 4
        sbs = [{} for _ in groups]
        S, X, V = sb_scores, sb_suffix, sb_values
        W, N, O = mlstm_weights, mlstm_numerators, mlstm_outputs
        order = [(S, 0), (W,), (X, 0), (S, 1), (V, 0), (X, 1), (S, 2), (N,),
                 (V, 1), (X, 2), (S, 3), (V, 2), (X, 3), (O,), (V, 3)]
        for fn, *group in order:
            if group:
                fn(groups[group[0]], sbs[group[0]])
            else:
                fn(st)

        def cond(state):
            j, alive_h = state
            return jnp.logical_and(j >= 0, alive_h > 0)

        for h in heads:
            def body(state, h=h):
                j, _ = state
                z = scores_z(h, keys(h, j, 1))
                nlk = softplus2(z)
                carry = carry_ref[h][:, 0:1]
                a = jnp.exp2(z - nlk + suffix(nlk) + carry)
                acc_ref[h] += jnp.dot(a.astype(BF16), values(h, j, 1), preferred_element_type=F32)
                carry = carry + total(nlk)
                carry_ref[h] = jnp.broadcast_to(carry, (tq, LANES))
                return j - 1, (jnp.max(carry) > UNDERFLOW_LOG2).astype(jnp.int32)

            lax.while_loop(cond, body, (c - 2, alive[h]))

    for h in heads:
        out_ref[:, sl[h]] = (ya_ref[:, sl[h]] + gb_ref[:, sl[h]].astype(F32) * acc_ref[h]).astype(out_ref.dtype)


def _mixers(p, gate_a, gates, gates_t, conv_w, bif_row, bif_col, norm_g, csm, B, S, L):
    T = B * S
    D = N_HEADS * HEAD_DIM
    nc = S // L
    tok = lambda blk: pl.BlockSpec((L, D), lambda b, c: (b * nc + c, blk))
    seq = lambda blk: pl.BlockSpec((S, D), lambda b, c: (b, blk), pipeline_mode=pl.Buffered(1))
    const = lambda shape: pl.BlockSpec(shape, lambda b, c: (0,) * len(shape))
    return pl.pallas_call(
        functools.partial(_mixers_kernel, L=L),
        grid=(B, nc),
        in_specs=[
            tok(PW_QK_A), tok(PW_V_A), tok(0),
            pl.BlockSpec((L, LANES), lambda b, c: (b * nc + c, 0)),
            pl.BlockSpec((1, 2 * N_HEADS, L), lambda b, c: (b * nc + c, 0, 0)),
            tok(PW_Q_B), seq(PW_K_B), seq(PW_V_B), tok(PW_GATE_B),
            const((CONV_K, D)), const((1, LANES)), const((2 * N_HEADS, 1)), const((1, D)),
            const((L, L)),
        ],
        out_specs=pl.BlockSpec((L, D), lambda b, c: (b * nc + c, 0)),
        out_shape=jax.ShapeDtypeStruct((T, D), BF16),
        scratch_shapes=[
            pltpu.VMEM((L + CONV_HALO, D), F32),
            pltpu.VMEM((N_HEADS, LANES, 2 * HEAD_DIM), F32),
            pltpu.VMEM((1, LANES), F32),
            pltpu.VMEM((L, D), F32),
            pltpu.VMEM((N_HEADS, L, HEAD_DIM), F32),
            pltpu.VMEM((N_HEADS, L, LANES), F32),
        ],
        compiler_params=pltpu.CompilerParams(
            dimension_semantics=("parallel", "arbitrary"), vmem_limit_bytes=VMEM_LIMIT),
        name="mixers",
    )(p, p, gate_a, gates, gates_t, p, p, p, p, conv_w, bif_row, bif_col, norm_g, csm)


def _mlp_kernel(x_ref, y_ref, wo_ref, g_ref, wu_ref, wd_ref, out_ref, h_ref):
    f = pl.program_id(1)

    @pl.when(f == 0)
    def _():
        x1 = x_ref[...] + jnp.dot(y_ref[...], wo_ref[...], preferred_element_type=F32)
        out_ref[...] = x1
        ms = jnp.mean(x1 * x1, axis=-1, keepdims=True)
        h_ref[...] = (x1 * lax.rsqrt(ms + EPS) * g_ref[...]).astype(BF16)

    u = jnp.maximum(jnp.dot(h_ref[...], wu_ref[...], preferred_element_type=F32), 0.0)
    out_ref[...] += jnp.dot((u * u).astype(BF16), wd_ref[...], preferred_element_type=F32)


def _outproj_mlp(x2d, y, w_out, g, w_up, w_down, tm, tf):
    T, D = x2d.shape
    F = w_up.shape[1]
    return pl.pallas_call(
        _mlp_kernel,
        grid=(T // tm, F // tf),
        in_specs=[
            pl.BlockSpec((tm, D), lambda i, f: (i, 0)),
            pl.BlockSpec((tm, D), lambda i, f: (i, 0)),
            pl.BlockSpec((D, D), lambda i, f: (0, 0)),
            pl.BlockSpec((1, D), lambda i, f: (0, 0)),
            pl.BlockSpec((D, tf), lambda i, f: (0, f)),
            pl.BlockSpec((tf, D), lambda i, f: (f, 0)),
        ],
        out_specs=pl.BlockSpec((tm, D), lambda i, f: (i, 0)),
        out_shape=jax.ShapeDtypeStruct((T, D), F32),
        scratch_shapes=[pltpu.VMEM((tm, D), BF16)],
        compiler_params=pltpu.CompilerParams(
            dimension_semantics=("parallel", "arbitrary"), vmem_limit_bytes=VMEM_LIMIT),
        name="outproj_mlp",
    )(x2d, y, w_out, g, w_up, w_down)


def _tiles(B, S):
    T = B * S
    return dict(
        tm_in=min(1024, T),
        chunk=min(256, S),
        tm_mlp=min(1024, T),
        tf=1024,
    )


def kernel(x, norm_mix_g, w_in, b_if, b_gate, conv_w, mlstm_norm_g, sb_q_norm_g, sb_k_norm_g,
           w_out, norm_mlp_g, w_up, w_down):
    B, S, D = x.shape
    depth = w_in.shape[0]
    T = B * S
    assert D == N_HEADS * HEAD_DIM and w_in.shape[2] == 8 * D + 2 * N_HEADS
    cfg = _tiles(B, S)
    tq = cfg["chunk"]

    r = lax.broadcasted_iota(jnp.int32, (tq, tq), 0)
    c = lax.broadcasted_iota(jnp.int32, (tq, tq), 1)
    csm = jnp.where(r > c, -1.0, 0.0).astype(BF16)

    w_main_all, w_gate_all = _wprep(jnp.swapaxes(w_in, 1, 2))

    x2d = x.reshape(T, D)
    for l in range(depth):
        qk_gain = jnp.concatenate([jnp.tile(sb_q_norm_g[l], N_HEADS),
                                   jnp.tile(sb_k_norm_g[l] * (HEAD_DIM ** -0.5 * LOG2_E), N_HEADS)]
                                  ).reshape(1, 2 * D)
        p, gate_a, gates, gates_t = _inproj(
            x2d, norm_mix_g[l].reshape(1, D), w_main_all, w_gate_all, l, qk_gain,
            b_gate[l][:D].reshape(1, D), b_gate[l][D:].reshape(1, D), cfg["tm_in"])
        bif_row = jnp.pad(b_if[l], (0, LANES - 2 * N_HEADS)).reshape(1, LANES)
        bif_col = b_if[l].reshape(2 * N_HEADS, 1)
        L = cfg["chunk"]
        gates_t = gates_t.reshape(2 * N_HEADS, T // L, L).transpose(1, 0, 2)
        y = _mixers(p, gate_a, gates, gates_t, conv_w[l], bif_row, bif_col, mlstm_norm_g[l].reshape(1, D),
                    csm, B, S, L)
        x2d = _outproj_mlp(x2d, y, w_out[l].astype(BF16), norm_mlp_g[l].reshape(1, D),
                           w_up[l].astype(BF16), w_down[l].astype(BF16), cfg["tm_mlp"], cfg["tf"])
    return x2d.reshape(B, S, D)
```

```python
import functools

import jax
import jax.numpy as jnp
from jax import lax
from jax.experimental import pallas as pl
from jax.experimental.pallas import tpu as pltpu

F32 = jnp.float32
BF16 = jnp.bfloat16

EPS = 1e-6
N_HEADS = 8
HEAD_DIM = 128
DQK_A = 64
CONV_K = 4
LANES = 128
CONV_HALO = 8
VMEM_LIMIT = 56 * 1024 * 1024
LOG2_E = 1.4426950408889634
UNDERFLOW_LOG2 = -110.0 * LOG2_E

W_IN_BLOCKS = (0, 1, 2, 6, 7, 5, 3, 4)
N_BLK = 8
INPROJ_TN = 2
SB_GROUP = 2
PW_QK_A, PW_V_A, PW_GATE_B, PW_V_B, PW_Q_B, PW_K_B = range(6)


def _log_sigmoid(x):
    return jnp.minimum(x, 0.0) - jnp.log(1.0 + jnp.exp2(jnp.abs(x) * -LOG2_E))


def _exp_neg(x):
    return jnp.exp2(x * -LOG2_E)


def _sigmoid(x):
    return 1.0 / (1.0 + _exp_neg(x))


def _wprep_kernel(w_ref, wg_ref, main_ref, gate_ref):
    main_ref[0] = w_ref[0].astype(main_ref.dtype)
    g = wg_ref[0]
    row = lax.broadcasted_iota(jnp.int32, g.shape, 0)
    gate_ref[0] = jnp.where(row < 2 * N_HEADS, g, 0.0).astype(gate_ref.dtype)


def _wprep(w_in_t):
    depth, n_in, D = w_in_t.shape
    n_gate = 2 * N_HEADS
    gate_lo = 3 * D

    def block_row(b):
        pos = functools.reduce(lambda acc, kv: jnp.where(b == kv[0], kv[1], acc), enumerate(W_IN_BLOCKS), 0)
        return pl.multiple_of(pos * D + jnp.where(pos >= 3, n_gate, 0), n_gate)

    return pl.pallas_call(
        _wprep_kernel,
        grid=(depth, N_BLK),
        in_specs=[pl.BlockSpec((pl.Element(1), pl.Element(D), pl.Element(D)), lambda l, b: (l, block_row(b), 0)),
                  pl.BlockSpec((pl.Element(1), pl.Element(LANES), pl.Element(D)), lambda l, b: (l, gate_lo, 0))],
        out_specs=[pl.BlockSpec((1, D, D), lambda l, b: (l, b, 0)),
                   pl.BlockSpec((1, LANES, D), lambda l, b: (l, 0, 0))],
        out_shape=[jax.ShapeDtypeStruct((depth, N_BLK * D, D), BF16),
                   jax.ShapeDtypeStruct((depth, LANES, D), BF16)],
        compiler_params=pltpu.CompilerParams(
            dimension_semantics=("parallel", "arbitrary"), vmem_limit_bytes=VMEM_LIMIT),
        name="inproj_weights",
    )(w_in_t, w_in_t)


def _dot_nt(a, b_t):
    return lax.dot_general(a, b_t, (((1,), (1,)), ((), ())), preferred_element_type=F32)


def _inproj_kernel(x_ref, g_ref, w_ref, wg_ref, qkg_ref, bga_ref, bgb_ref,
                   p_ref, ga_ref, gates_ref, gates_t_ref, h_ref):
    j = pl.program_id(1)
    D = x_ref.shape[1]

    @pl.when(j == 0)
    def _():
        x = x_ref[...]
        ms = jnp.mean(x * x, axis=-1, keepdims=True)
        h = (x * lax.rsqrt(ms + EPS) * g_ref[...]).astype(BF16)
        h_ref[...] = h
        gates = _dot_nt(h, wg_ref[0])
        gates_ref[...] = gates
        gates_t_ref[...] = gates.T[:2 * N_HEADS, :]
        acc = _dot_nt(h, w_ref[0])
        p_ref[...] = acc.astype(p_ref.dtype)

    @pl.when(j == 1)
    def _():
        acc = _dot_nt(h_ref[...], w_ref[0])
        gate = 1.0 / ((1.0 + _exp_neg(acc[:, :D])) * (1.0 + _exp_neg(acc[:, D:] + bga_ref[...])))
        ga_ref[...] = gate.astype(ga_ref.dtype)

    @pl.when(j == 2)
    def _():
        acc = _dot_nt(h_ref[...], w_ref[0])
        p_ref[:, :D] = _sigmoid(acc[:, :D] + bgb_ref[...]).astype(p_ref.dtype)
        p_ref[:, D:] = acc[:, D:].astype(p_ref.dtype)

    @pl.when(j == 3)
    def _():
        acc = _dot_nt(h_ref[...], w_ref[0])
        gain = qkg_ref[...]
        for h in range(2 * N_HEADS):
            sl = slice(h * HEAD_DIM, (h + 1) * HEAD_DIM)
            a = acc[:, sl]
            ms = jnp.mean(a * a, axis=-1, keepdims=True)
            p_ref[:, sl] = (a * lax.rsqrt(ms + EPS) * gain[:, sl]).astype(p_ref.dtype)


def _inproj(x2d, g, w_main, w_gate, layer, qk_gain, b_gate_a, b_gate_b, tm):
    T, D = x2d.shape
    tn = INPROJ_TN * D
    n_tiles = w_main.shape[1] // tn
    assert n_tiles == 4 and qk_gain.shape == (1, tn)
    const = lambda shape: pl.BlockSpec(shape, lambda i, j: (0,) * len(shape))
    return pl.pallas_call(
        _inproj_kernel,
        grid=(T // tm, n_tiles),
        in_specs=[
            pl.BlockSpec((tm, D), lambda i, j: (i, 0)),
            const((1, D)),
            pl.BlockSpec((1, tn, D), lambda i, j: (layer, j, 0)),
            pl.BlockSpec((1, LANES, D), lambda i, j: (layer, 0, 0)),
            const((1, tn)), const((1, D)), const((1, D)),
        ],
        out_specs=[
            pl.BlockSpec((tm, tn), lambda i, j: (i, jnp.maximum(j - 1, 0))),
            pl.BlockSpec((tm, D), lambda i, j: (i, 0)),
            pl.BlockSpec((tm, LANES), lambda i, j: (i, 0)),
            pl.BlockSpec((2 * N_HEADS, tm), lambda i, j: (0, i)),
        ],
        out_shape=[
            jax.ShapeDtypeStruct((T, (n_tiles - 1) * tn), BF16),
            jax.ShapeDtypeStruct((T, D), BF16),
            jax.ShapeDtypeStruct((T, LANES), F32),
            jax.ShapeDtypeStruct((2 * N_HEADS, T), F32),
        ],
        scratch_shapes=[pltpu.VMEM((tm, D), BF16)],
        compiler_params=pltpu.CompilerParams(
            dimension_semantics=("parallel", "arbitrary"), vmem_limit_bytes=VMEM_LIMIT),
        name="norm_inproj",
    )(x2d, g, w_main, w_gate, qk_gain, b_gate_a, b_gate_b)


def _mixers_kernel(qk_ref, va_ref, ga_ref, gates_ref, gates_t_ref, q_ref, k_ref, v_ref, gb_ref,
                   convw_ref, bif_row_ref, bif_col_ref, ng_ref, csm_ref,
                   out_ref, xbuf_ref, c_ref, m_ref, ya_ref, acc_ref, carry_ref, *, L):
    c = pl.program_id(1)
    d_qk = N_HEADS * DQK_A
    heads = range(N_HEADS)
    sl = [slice(h * HEAD_DIM, (h + 1) * HEAD_DIM) for h in heads]
    tq = L
    csm = csm_ref[...]

    def keys(h, j, n):
        return k_ref[pl.ds(pl.multiple_of(j * tq, tq), n * tq), sl[h]]

    def values(h, j, n):
        return v_ref[pl.ds(pl.multiple_of(j * tq, tq), n * tq), sl[h]]

    def scores_z(h, kj):
        return lax.dot_general(q_ref[:, sl[h]], kj, (((1,), (1,)), ((), ())), preferred_element_type=F32)

    def softplus2(z):
        return jnp.maximum(z, 0.0) + jnp.log2(1.0 + jnp.exp2(-jnp.abs(z)))

    def suffix(nlk):
        return jnp.dot(nlk.astype(BF16), csm, preferred_element_type=F32)

    def total(nlk):
        return -jnp.sum(nlk, axis=1, keepdims=True)

    def strict_mask():
        row = lax.broadcasted_iota(jnp.int32, (tq, tq), 0)
        col = lax.broadcasted_iota(jnp.int32, (tq, tq), 1)
        return col < row

    def head_col(x, h):
        return x[:, N_HEADS + h:N_HEADS + h + 1]

    def write_out(h, acc):
        out_ref[:, sl[h]] = (ya_ref[:, sl[h]] + gb_ref[:, sl[h]].astype(F32) * acc).astype(out_ref.dtype)

    def mlstm_prepare(st):
        xbuf_ref[CONV_HALO:CONV_HALO + L, :] = qk_ref[...].astype(F32)
        w = convw_ref[...]
        y = w[CONV_K - 1:CONV_K, :] * xbuf_ref[CONV_HALO:CONV_HALO + L, :]
        for tap in range(CONV_K - 1):
            off = CONV_HALO - (CONV_K - 1) + tap
            y = y + w[tap:tap + 1, :] * xbuf_ref[off:off + L, :]
        xbuf_ref[0:CONV_HALO, :] = xbuf_ref[L:L + CONV_HALO, :]
        qk = y * _sigmoid(y)

        row = lax.broadcasted_iota(jnp.int32, (L, L), 0)
        col = lax.broadcasted_iota(jnp.int32, (L, L), 1)
        st["causal"] = causal = col <= row
        tri = jnp.where(causal, 1.0, 0.0).astype(F32)
        pre_col = gates_ref[...] + bif_row_ref[...]
        b_col_all = jnp.dot(tri, _log_sigmoid(pre_col), preferred_element_type=F32,
                            precision=lax.Precision.HIGHEST)
        st["pre_row"] = pre_row = gates_t_ref[0] + bif_col_ref[...]
        st["b_row_all"] = lax.dot_general(_log_sigmoid(pre_row), tri, (((1,), (1,)), ((), ())),
                                          preferred_element_type=F32, precision=lax.Precision.HIGHEST)

        lane = lax.broadcasted_iota(jnp.int32, (L, LANES), 1)
        head_lanes = jnp.logical_and(lane >= N_HEADS, lane < 2 * N_HEADS)
        b_all = jnp.where(head_lanes, b_col_all, 0.0)
        i_all = jnp.where(head_lanes, pltpu.roll(pre_col, N_HEADS, axis=1), 0.0)
        rowi = lax.broadcasted_iota(jnp.int32, (L, LANES), 0)
        g_max = i_all - b_all
        shift = 1
        while shift < L:
            g_max = jnp.maximum(g_max, jnp.where(rowi >= shift, pltpu.roll(g_max, shift, axis=0), -jnp.inf))
            shift *= 2
        m_prev = m_ref[...]
        st["m_rel"] = m_rel = jnp.maximum(m_prev, g_max)
        st["inter_all"] = jnp.exp(m_prev - m_rel)
        st["floor_all"] = jnp.exp(-(b_all + m_rel))
        b_last = b_all[L - 1:L, :]
        a_all = b_last - b_all + i_all
        m_loc = jnp.max(a_all, axis=0, keepdims=True)
        w_all = jnp.exp(a_all - m_loc)
        m_new = jnp.maximum(b_last + m_prev, m_loc)
        st["sp_all"] = jnp.exp(b_last + m_prev - m_new)
        st["sc_all"] = jnp.exp(m_loc - m_new)
        m_ref[...] = m_new

        q_pair, km_b, vh, c0, s_qk, q_c, c_loc = ({} for _ in range(7))
        for h in heads:
            pair, half = divmod(h, 2)
            q_pair[h] = qk[:, pair * LANES:(pair + 1) * LANES].astype(BF16)
            k_pair = qk[:, d_qk + pair * LANES:d_qk + (pair + 1) * LANES]
            in_head = jnp.logical_and(lane >= half * DQK_A, lane < (half + 1) * DQK_A)
            km_b[h] = jnp.where(in_head, k_pair * (DQK_A ** -0.5), 0.0).astype(BF16)
            vh[h] = va_ref[:, sl[h]]
            c0[h] = c_ref[h]
        for h in heads:
            s_qk[h] = lax.dot_general(q_pair[h], km_b[h], (((1,), (1,)), ((), ())), preferred_element_type=F32)
        for h in heads:
            q_c[h] = jnp.dot(q_pair[h], c0[h].astype(BF16), preferred_element_type=F32)
        for h in heads:
            w_t = head_col(w_all, h)
            wv = jnp.concatenate([w_t * vh[h].astype(F32), jnp.broadcast_to(w_t, (L, HEAD_DIM))],
                                 axis=1).astype(BF16)
            c_loc[h] = lax.dot_general(km_b[h], wv, (((0,), (0,)), ((), ())), preferred_element_type=F32)
        st.update(vh=vh, c0=c0, s_qk=s_qk, q_c=q_c, c_loc=c_loc)

    def mlstm_weights(st):
        p = {}
        for h in heads:
            g_s = st["pre_row"][h:h + 1, :] - st["b_row_all"][N_HEADS + h:N_HEADS + h + 1, :]
            p[h] = jnp.exp(jnp.where(st["causal"], g_s - head_col(st["m_rel"], h), -jnp.inf)) * st["s_qk"][h]
        st["p"] = p

    def mlstm_numerators(st):
        ones_cols = jnp.ones((L, HEAD_DIM), BF16)
        st["p_v"] = {h: jnp.dot(st["p"][h].astype(BF16), jnp.concatenate([st["vh"][h], ones_cols], axis=1),
                                preferred_element_type=F32) for h in heads}

    def mlstm_outputs(st):
        mean_cols = jnp.full((HEAD_DIM, HEAD_DIM), 1.0 / HEAD_DIM, BF16)
        for h in heads:
            both = st["p_v"][h] + head_col(st["inter_all"], h) * st["q_c"][h]
            num, den = both[:, :HEAD_DIM], both[:, HEAD_DIM:]
            hh = num * (1.0 / jnp.maximum(jnp.abs(den), head_col(st["floor_all"], h)))
            ms = jnp.dot((hh * hh).astype(BF16), mean_cols, preferred_element_type=F32)
            hn = hh * lax.rsqrt(ms + EPS) * ng_ref[:, sl[h]]
            ya_ref[:, sl[h]] = ga_ref[:, sl[h]].astype(F32) * hn
        for h in heads:
            c_ref[h] = head_col(st["sp_all"], h) * st["c0"][h] + head_col(st["sc_all"], h) * st["c_loc"][h]

    @pl.when(c == 0)
    def _():
        xbuf_ref[0:CONV_HALO, :] = jnp.zeros((CONV_HALO, xbuf_ref.shape[1]), F32)
        c_ref[...] = jnp.zeros(c_ref.shape, F32)
        m_ref[...] = jnp.zeros(m_ref.shape, F32)
        st = {}
        mlstm_prepare(st)
        strict = strict_mask()
        z_all = {h: scores_z(h, keys(h, 0, 1)) for h in heads}
        mlstm_weights(st)
        d_all, nlk_all = {}, {}
        for h in heads:
            nlk = softplus2(z_all[h])
            d_all[h] = z_all[h] - nlk
            nlk_all[h] = jnp.where(strict, nlk, 0.0)
        cs = {h: suffix(nlk_all[h]) for h in heads}
        mlstm_numerators(st)
        a_all = {h: jnp.where(strict, jnp.exp2(d_all[h] + cs[h]), 0.0).astype(BF16) for h in heads}
        acc0 = {h: jnp.dot(a_all[h], values(h, 0, 1), preferred_element_type=F32) for h in heads}
        mlstm_outputs(st)
        for h in heads:
            write_out(h, acc0[h])

    @pl.when(c > 0)
    def _():
        st = {}
        mlstm_prepare(st)
        strict = strict_mask()
        alive = [None] * N_HEADS
        def sb_scores(hs, sb):
            sb["z"] = {h: scores_z(h, keys(h, c - 1, 2)) for h in hs}

        def sb_suffix(hs, sb):
            sb["d"], sb["nd"], sb["np"] = {}, {}, {}
            for h in hs:
                nlk = softplus2(sb["z"][h])
                sb["d"][h] = sb["z"][h] - nlk
                sb["nd"][h] = jnp.where(strict, nlk[:, tq:], 0.0)
                sb["np"][h] = nlk[:, :tq]
            sb["cd"] = {h: suffix(sb["nd"][h]) for h in hs}
            sb["cp"] = {h: suffix(sb["np"][h]) for h in hs}

        def sb_values(hs, sb):
            carry_d = {h: total(sb["nd"][h]) for h in hs}
            a_all = {}
            for h in hs:
                a_d = jnp.where(strict, jnp.exp2(sb["d"][h][:, tq:] + sb["cd"][h]), 0.0)
                a_p = jnp.exp2(sb["d"][h][:, :tq] + sb["cp"][h] + carry_d[h])
                a_all[h] = jnp.concatenate([a_p, a_d], axis=1).astype(BF16)
            pv = {h: jnp.dot(a_all[h], values(h, c - 1, 2), preferred_element_type=F32) for h in hs}
            for h in hs:
                acc_ref[h] = pv[h]
                sb.setdefault("pv", {})[h] = pv[h]
                carry = carry_d[h] + total(sb["np"][h])
                carry_ref[h] = jnp.broadcast_to(carry, (tq, LANES))
                alive[h] = (jnp.max(carry) > UNDERFLOW_LOG2).astype(jnp.int32)

        groups = [range(g, g + SB_GROUP) for g in range(0, N_HEADS, SB_GROUP)]
        assert len(groups) == 4
        sbs = [{} for _ in groups]
        S, X, V = sb_scores, sb_suffix, sb_values
        W, N, O = mlstm_weights, mlstm_numerators, mlstm_outputs
        order = [(S, 0), (W,), (X, 0), (S, 1), (V, 0), (X, 1), (S, 2), (N,),
                 (V, 1), (X, 2), (S, 3), (V, 2), (X, 3), (O,), (V, 3)]
        for fn, *group in order:
            if group:
                fn(groups[group[0]], sbs[group[0]])
            else:
                fn(st)
        for hs, sb in zip(groups, sbs):
            for h in hs:
                write_out(h, sb["pv"][h])

        def cond(state):
            j, alive_h = state
            return jnp.logical_and(j >= 0, alive_h > 0)

        for h in heads:
            def body(state, h=h):
                j, _ = state
                z = scores_z(h, keys(h, j, 1))
                nlk = softplus2(z)
                carry = carry_ref[h][:, 0:1]
                a = jnp.exp2(z - nlk + suffix(nlk) + carry)
                acc = acc_ref[h] + jnp.dot(a.astype(BF16), values(h, j, 1), preferred_element_type=F32)
                acc_ref[h] = acc
                write_out(h, acc)
                carry = carry + total(nlk)
                carry_ref[h] = jnp.broadcast_to(carry, (tq, LANES))
                return j - 1, (jnp.max(carry) > UNDERFLOW_LOG2).astype(jnp.int32)

            lax.while_loop(cond, body, (c - 2, alive[h]))


def _mixers(p, gate_a, gates, gates_t, conv_w, bif_row, bif_col, norm_g, csm, B, S, L):
    T = B * S
    D = N_HEADS * HEAD_DIM
    nc = S // L
    tok = lambda blk: pl.BlockSpec((L, D), lambda b, c: (b * nc + c, blk))
    seq = lambda blk: pl.BlockSpec((S, D), lambda b, c: (b, blk), pipeline_mode=pl.Buffered(1))
    const = lambda shape: pl.BlockSpec(shape, lambda b, c: (0,) * len(shape))
    return pl.pallas_call(
        functools.partial(_mixers_kernel, L=L),
        grid=(B, nc),
        in_specs=[
            tok(PW_QK_A), tok(PW_V_A), tok(0),
            pl.BlockSpec((L, LANES), lambda b, c: (b * nc + c, 0)),
            pl.BlockSpec((1, 2 * N_HEADS, L), lambda b, c: (b * nc + c, 0, 0)),
            tok(PW_Q_B), seq(PW_K_B), seq(PW_V_B), tok(PW_GATE_B),
            const((CONV_K, D)), const((1, LANES)), const((2 * N_HEADS, 1)), const((1, D)),
            const((L, L)),
        ],
        out_specs=pl.BlockSpec((L, D), lambda b, c: (b * nc + c, 0)),
        out_shape=jax.ShapeDtypeStruct((T, D), BF16),
        scratch_shapes=[
            pltpu.VMEM((L + CONV_HALO, D), F32),
            pltpu.VMEM((N_HEADS, LANES, 2 * HEAD_DIM), F32),
            pltpu.VMEM((1, LANES), F32),
            pltpu.VMEM((L, D), F32),
            pltpu.VMEM((N_HEADS, L, HEAD_DIM), F32),
            pltpu.VMEM((N_HEADS, L, LANES), F32),
        ],
        compiler_params=pltpu.CompilerParams(
            dimension_semantics=("parallel", "arbitrary"), vmem_limit_bytes=VMEM_LIMIT),
        name="mixers",
    )(p, p, gate_a, gates, gates_t, p, p, p, p, conv_w, bif_row, bif_col, norm_g, csm)


def _mlp_kernel(x_ref, y_ref, wo_ref, g_ref, wu_ref, wd_ref, out_ref, *, tf):
    x1 = x_ref[...] + jnp.dot(y_ref[...], wo_ref[...], preferred_element_type=F32)
    ms = jnp.mean(x1 * x1, axis=-1, keepdims=True)
    h = (x1 * lax.rsqrt(ms + EPS) * g_ref[...]).astype(BF16)
    out_ref[...] = x1
    for f in range(wu_ref.shape[1] // tf):
        u = jnp.maximum(jnp.dot(h, wu_ref[:, f * tf:(f + 1) * tf], preferred_element_type=F32), 0.0)
        out_ref[...] += jnp.dot((u * u).astype(BF16), wd_ref[f * tf:(f + 1) * tf, :], preferred_element_type=F32)


def _outproj_mlp(x2d, y, w_out, g, w_up, w_down, tm, tf):
    T, D = x2d.shape
    F = w_up.shape[1]
    resident = lambda shape: pl.BlockSpec(shape, lambda i: (0,) * len(shape), pipeline_mode=pl.Buffered(1))
    return pl.pallas_call(
        functools.partial(_mlp_kernel, tf=tf),
        grid=(T // tm,),
        in_specs=[
            pl.BlockSpec((tm, D), lambda i: (i, 0)),
            pl.BlockSpec((tm, D), lambda i: (i, 0)),
            resident((D, D)),
            pl.BlockSpec((1, D), lambda i: (0, 0)),
            resident((D, F)),
            resident((F, D)),
        ],
        out_specs=pl.BlockSpec((tm, D), lambda i: (i, 0)),
        out_shape=jax.ShapeDtypeStruct((T, D), F32),
        compiler_params=pltpu.CompilerParams(
            dimension_semantics=("parallel",), vmem_limit_bytes=VMEM_LIMIT),
        name="outproj_mlp",
    )(x2d, y, w_out, g, w_up, w_down)


def _tiles(B, S):
    T = B * S
    return dict(
        tm_in=min(1024, T),
        chunk=min(256, S),
        tm_mlp=min(1024, T),
        tf=1024,
    )


def kernel(x, norm_mix_g, w_in, b_if, b_gate, conv_w, mlstm_norm_g, sb_q_norm_g, sb_k_norm_g,
           w_out, norm_mlp_g, w_up, w_down):
    B, S, D = x.shape
    depth = w_in.shape[0]
    T = B * S
    assert D == N_HEADS * HEAD_DIM and w_in.shape[2] == 8 * D + 2 * N_HEADS
    cfg = _tiles(B, S)
    tq = cfg["chunk"]

    r = lax.broadcasted_iota(jnp.int32, (tq, tq), 0)
    c = lax.broadcasted_iota(jnp.int32, (tq, tq), 1)
    csm = jnp.where(r > c, -1.0, 0.0).astype(BF16)

    w_main_all, w_gate_all = _wprep(jnp.swapaxes(w_in, 1, 2))

    x2d = x.reshape(T, D)
    for l in range(depth):
        qk_gain = jnp.concatenate([jnp.tile(sb_q_norm_g[l], N_HEADS),
                                   jnp.tile(sb_k_norm_g[l] * (HEAD_DIM ** -0.5 * LOG2_E), N_HEADS)]
                                  ).reshape(1, 2 * D)
        p, gate_a, gates, gates_t = _inproj(
            x2d, norm_mix_g[l].reshape(1, D), w_main_all, w_gate_all, l, qk_gain,
            b_gate[l][:D].reshape(1, D), b_gate[l][D:].reshape(1, D), cfg["tm_in"])
        bif_row = jnp.pad(b_if[l], (0, LANES - 2 * N_HEADS)).reshape(1, LANES)
        bif_col = b_if[l].reshape(2 * N_HEADS, 1)
        L = cfg["chunk"]
        gates_t = gates_t.reshape(2 * N_HEADS, T // L, L).transpose(1, 0, 2)
        y = _mixers(p, gate_a, gates, gates_t, conv_w[l], bif_row, bif_col, mlstm_norm_g[l].reshape(1, D),
                    csm, B, S, L)
        x2d = _outproj_mlp(x2d, y, w_out[l].astype(BF16), norm_mlp_g[l].reshape(1, D),
                           w_up[l].astype(BF16), w_down[l].astype(BF16), cfg["tm_mlp"], cfg["tf"])
    return x2d.reshape(B, S, D)
```

```python
import functools

import jax
import jax.numpy as jnp
from jax import lax
from jax.experimental import pallas as pl
from jax.experimental.pallas import tpu as pltpu

F32 = jnp.float32
BF16 = jnp.bfloat16

EPS = 1e-6
N_HEADS = 8
HEAD_DIM = 128
DQK_A = 64
CONV_K = 4
LANES = 128
CONV_HALO = 8
VMEM_LIMIT = 56 * 1024 * 1024
LOG2_E = 1.4426950408889634
UNDERFLOW_LOG2 = -110.0 * LOG2_E

W_IN_BLOCKS = (0, 1, 2, 6, 7, 5, 3, 4)
N_BLK = 8
INPROJ_TN = 2
SB_GROUP = 2
PW_QK_A, PW_V_A, PW_GATE_B, PW_V_B, PW_Q_B, PW_K_B = range(6)


def _log_sigmoid(x):
    return jnp.minimum(x, 0.0) - jnp.log(1.0 + jnp.exp2(jnp.abs(x) * -LOG2_E))


def _exp_neg(x):
    return jnp.exp2(x * -LOG2_E)


def _sigmoid(x):
    return 1.0 / (1.0 + _exp_neg(x))


def _wprep_kernel(w_ref, wg_ref, main_ref, gate_ref):
    main_ref[0] = w_ref[0].astype(main_ref.dtype)
    g = wg_ref[0]
    row = lax.broadcasted_iota(jnp.int32, g.shape, 0)
    gate_ref[0] = jnp.where(row < 2 * N_HEADS, g, 0.0).astype(gate_ref.dtype)


def _wprep(w_in_t):
    depth, n_in, D = w_in_t.shape
    n_gate = 2 * N_HEADS
    gate_lo = 3 * D

    def block_row(b):
        pos = functools.reduce(lambda acc, kv: jnp.where(b == kv[0], kv[1], acc), enumerate(W_IN_BLOCKS), 0)
        return pl.multiple_of(pos * D + jnp.where(pos >= 3, n_gate, 0), n_gate)

    return pl.pallas_call(
        _wprep_kernel,
        grid=(depth, N_BLK),
        in_specs=[pl.BlockSpec((pl.Element(1), pl.Element(D), pl.Element(D)), lambda l, b: (l, block_row(b), 0)),
                  pl.BlockSpec((pl.Element(1), pl.Element(LANES), pl.Element(D)), lambda l, b: (l, gate_lo, 0))],
        out_specs=[pl.BlockSpec((1, D, D), lambda l, b: (l, b, 0)),
                   pl.BlockSpec((1, LANES, D), lambda l, b: (l, 0, 0))],
        out_shape=[jax.ShapeDtypeStruct((depth, N_BLK * D, D), BF16),
                   jax.ShapeDtypeStruct((depth, LANES, D), BF16)],
        compiler_params=pltpu.CompilerParams(
            dimension_semantics=("parallel", "arbitrary"), vmem_limit_bytes=VMEM_LIMIT),
        name="inproj_weights",
    )(w_in_t, w_in_t)


def _dot_nt(a, b_t):
    return lax.dot_general(a, b_t, (((1,), (1,)), ((), ())), preferred_element_type=F32)


def _inproj_kernel(x_ref, g_ref, w_ref, wg_ref, qkg_ref, bga_ref, bgb_ref,
                   p_ref, ga_ref, gates_ref, gates_t_ref):
    D = x_ref.shape[1]
    tn = INPROJ_TN * D
    x = x_ref[...]
    ms = jnp.mean(x * x, axis=-1, keepdims=True)
    h = (x * lax.rsqrt(ms + EPS) * g_ref[...]).astype(BF16)
    gates = _dot_nt(h, wg_ref[0])
    gates_ref[...] = gates
    gates_t_ref[...] = gates.T[:2 * N_HEADS, :]

    acc = _dot_nt(h, w_ref[0, 0 * tn:1 * tn, :])
    p_ref[:, 0:tn] = acc.astype(p_ref.dtype)

    acc = _dot_nt(h, w_ref[0, 1 * tn:2 * tn, :])
    gate = 1.0 / ((1.0 + _exp_neg(acc[:, :D])) * (1.0 + _exp_neg(acc[:, D:] + bga_ref[...])))
    ga_ref[...] = gate.astype(ga_ref.dtype)

    acc = _dot_nt(h, w_ref[0, 2 * tn:3 * tn, :])
    p_ref[:, tn:tn + D] = _sigmoid(acc[:, :D] + bgb_ref[...]).astype(p_ref.dtype)
    p_ref[:, tn + D:2 * tn] = acc[:, D:].astype(p_ref.dtype)

    acc = _dot_nt(h, w_ref[0, 3 * tn:4 * tn, :])
    gain = qkg_ref[...]
    for hd in range(2 * N_HEADS):
        sl = slice(hd * HEAD_DIM, (hd + 1) * HEAD_DIM)
        a = acc[:, sl]
        ms = jnp.mean(a * a, axis=-1, keepdims=True)
        p_ref[:, 2 * tn + hd * HEAD_DIM:2 * tn + (hd + 1) * HEAD_DIM] = (
            a * lax.rsqrt(ms + EPS) * gain[:, sl]).astype(p_ref.dtype)


def _inproj(x2d, g, w_main, w_gate, layer, qk_gain, b_gate_a, b_gate_b, tm):
    T, D = x2d.shape
    tn = INPROJ_TN * D
    n_rows = w_main.shape[1]
    assert n_rows == 4 * tn and qk_gain.shape == (1, tn)
    const = lambda shape: pl.BlockSpec(shape, lambda i: (0,) * len(shape))
    return pl.pallas_call(
        _inproj_kernel,
        grid=(T // tm,),
        in_specs=[
            pl.BlockSpec((tm, D), lambda i: (i, 0)),
            const((1, D)),
            pl.BlockSpec((1, n_rows, D), lambda i: (layer, 0, 0), pipeline_mode=pl.Buffered(1)),
            pl.BlockSpec((1, LANES, D), lambda i: (layer, 0, 0), pipeline_mode=pl.Buffered(1)),
            const((1, tn)), const((1, D)), const((1, D)),
        ],
        out_specs=[
            pl.BlockSpec((tm, 3 * tn), lambda i: (i, 0)),
            pl.BlockSpec((tm, D), lambda i: (i, 0)),
            pl.BlockSpec((tm, LANES), lambda i: (i, 0)),
            pl.BlockSpec((2 * N_HEADS, tm), lambda i: (0, i)),
        ],
        out_shape=[
            jax.ShapeDtypeStruct((T, 3 * tn), BF16),
            jax.ShapeDtypeStruct((T, D), BF16),
            jax.ShapeDtypeStruct((T, LANES), F32),
            jax.ShapeDtypeStruct((2 * N_HEADS, T), F32),
        ],
        compiler_params=pltpu.CompilerParams(
            dimension_semantics=("parallel",), vmem_limit_bytes=VMEM_LIMIT),
        name="norm_inproj",
    )(x2d, g, w_main, w_gate, qk_gain, b_gate_a, b_gate_b)


def _mixers_kernel(qk_ref, va_ref, ga_ref, gates_ref, gates_t_ref, q_ref, k_ref, v_ref, gb_ref,
                   convw_ref, bif_row_ref, bif_col_ref, ng_ref, csm_ref,
                   out_ref, xbuf_ref, c_ref, m_ref, ya_ref, acc_ref, carry_ref, *, L):
    c = pl.program_id(1)
    d_qk = N_HEADS * DQK_A
    heads = range(N_HEADS)
    sl = [slice(h * HEAD_DIM, (h + 1) * HEAD_DIM) for h in heads]
    tq = L
    csm = csm_ref[...]

    def keys(h, j, n):
        return k_ref[pl.ds(pl.multiple_of(j * tq, tq), n * tq), sl[h]]

    def values(h, j, n):
        return v_ref[pl.ds(pl.multiple_of(j * tq, tq), n * tq), sl[h]]

    def scores_z(h, kj):
        return lax.dot_general(q_ref[:, sl[h]], kj, (((1,), (1,)), ((), ())), preferred_element_type=F32)

    def softplus2(z):
        return jnp.maximum(z, 0.0) + jnp.log2(1.0 + jnp.exp2(-jnp.abs(z)))

    def suffix(nlk):
        return jnp.dot(nlk.astype(BF16), csm, preferred_element_type=F32)

    def total(nlk):
        return -jnp.sum(nlk, axis=1, keepdims=True)

    def strict_mask():
        row = lax.broadcasted_iota(jnp.int32, (tq, tq), 0)
        col = lax.broadcasted_iota(jnp.int32, (tq, tq), 1)
        return col < row

    def head_col(x, h):
        return x[:, N_HEADS + h:N_HEADS + h + 1]

    def write_out(h, acc):
        out_ref[:, sl[h]] = (ya_ref[:, sl[h]] + gb_ref[:, sl[h]].astype(F32) * acc).astype(out_ref.dtype)

    def mlstm_prepare(st):
        xbuf_ref[CONV_HALO:CONV_HALO + L, :] = qk_ref[...].astype(F32)
        w = convw_ref[...]
        y = w[CONV_K - 1:CONV_K, :] * xbuf_ref[CONV_HALO:CONV_HALO + L, :]
        for tap in range(CONV_K - 1):
            off = CONV_HALO - (CONV_K - 1) + tap
            y = y + w[tap:tap + 1, :] * xbuf_ref[off:off + L, :]
        xbuf_ref[0:CONV_HALO, :] = xbuf_ref[L:L + CONV_HALO, :]
        qk = y * _sigmoid(y)

        row = lax.broadcasted_iota(jnp.int32, (L, L), 0)
        col = lax.broadcasted_iota(jnp.int32, (L, L), 1)
        st["causal"] = causal = col <= row
        tri = jnp.where(causal, 1.0, 0.0).astype(F32)
        pre_col = gates_ref[...] + bif_row_ref[...]
        b_col_all = jnp.dot(tri, _log_sigmoid(pre_col), preferred_element_type=F32,
                            precision=lax.Precision.HIGHEST)
        st["pre_row"] = pre_row = gates_t_ref[0] + bif_col_ref[...]
        st["b_row_all"] = lax.dot_general(_log_sigmoid(pre_row), tri, (((1,), (1,)), ((), ())),
                                          preferred_element_type=F32, precision=lax.Precision.HIGHEST)

        lane = lax.broadcasted_iota(jnp.int32, (L, LANES), 1)
        head_lanes = jnp.logical_and(lane >= N_HEADS, lane < 2 * N_HEADS)
        b_all = jnp.where(head_lanes, b_col_all, 0.0)
        i_all = jnp.where(head_lanes, pltpu.roll(pre_col, N_HEADS, axis=1), 0.0)
        rowi = lax.broadcasted_iota(jnp.int32, (L, LANES), 0)
        g_max = i_all - b_all
        shift = 1
        while shift < L:
            g_max = jnp.maximum(g_max, jnp.where(rowi >= shift, pltpu.roll(g_max, shift, axis=0), -jnp.inf))
            shift *= 2
        m_prev = m_ref[...]
        st["m_rel"] = m_rel = jnp.maximum(m_prev, g_max)
        st["inter_all"] = jnp.exp(m_prev - m_rel)
        st["floor_all"] = jnp.exp(-(b_all + m_rel))
        b_last = b_all[L - 1:L, :]
        a_all = b_last - b_all + i_all
        m_loc = jnp.max(a_all, axis=0, keepdims=True)
        w_all = jnp.exp(a_all - m_loc)
        m_new = jnp.maximum(b_last + m_prev, m_loc)
        st["sp_all"] = jnp.exp(b_last + m_prev - m_new)
        st["sc_all"] = jnp.exp(m_loc - m_new)
        m_ref[...] = m_new

        q_pair, km_b, vh, c0, s_qk, q_c, c_loc = ({} for _ in range(7))
        for h in heads:
            pair, half = divmod(h, 2)
            q_pair[h] = qk[:, pair * LANES:(pair + 1) * LANES].astype(BF16)
            k_pair = qk[:, d_qk + pair * LANES:d_qk + (pair + 1) * LANES]
            in_head = jnp.logical_and(lane >= half * DQK_A, lane < (half + 1) * DQK_A)
            km_b[h] = jnp.where(in_head, k_pair * (DQK_A ** -0.5), 0.0).astype(BF16)
            vh[h] = va_ref[:, sl[h]]
            c0[h] = c_ref[h]
        for h in heads:
            s_qk[h] = lax.dot_general(q_pair[h], km_b[h], (((1,), (1,)), ((), ())), preferred_element_type=F32)
        for h in heads:
            q_c[h] = jnp.dot(q_pair[h], c0[h].astype(BF16), preferred_element_type=F32)
        for h in heads:
            w_t = head_col(w_all, h)
            wv = jnp.concatenate([w_t * vh[h].astype(F32), jnp.broadcast_to(w_t, (L, HEAD_DIM))],
                                 axis=1).astype(BF16)
            c_loc[h] = lax.dot_general(km_b[h], wv, (((0,), (0,)), ((), ())), preferred_element_type=F32)
        st.update(vh=vh, c0=c0, s_qk=s_qk, q_c=q_c, c_loc=c_loc)

    def mlstm_weights(st):
        p = {}
        for h in heads:
            g_s = st["pre_row"][h:h + 1, :] - st["b_row_all"][N_HEADS + h:N_HEADS + h + 1, :]
            p[h] = jnp.exp(jnp.where(st["causal"], g_s - head_col(st["m_rel"], h), -jnp.inf)) * st["s_qk"][h]
        st["p"] = p

    def mlstm_numerators(st):
        ones_cols = jnp.ones((L, HEAD_DIM), BF16)
        st["p_v"] = {h: jnp.dot(st["p"][h].astype(BF16), jnp.concatenate([st["vh"][h], ones_cols], axis=1),
                                preferred_element_type=F32) for h in heads}

    def mlstm_outputs(st):
        mean_cols = jnp.full((HEAD_DIM, HEAD_DIM), 1.0 / HEAD_DIM, BF16)
        for h in heads:
            both = st["p_v"][h] + head_col(st["inter_all"], h) * st["q_c"][h]
            num, den = both[:, :HEAD_DIM], both[:, HEAD_DIM:]
            hh = num * (1.0 / jnp.maximum(jnp.abs(den), head_col(st["floor_all"], h)))
            ms = jnp.dot((hh * hh).astype(BF16), mean_cols, preferred_element_type=F32)
            hn = hh * lax.rsqrt(ms + EPS) * ng_ref[:, sl[h]]
            ya_ref[:, sl[h]] = ga_ref[:, sl[h]].astype(F32) * hn
        for h in heads:
            c_ref[h] = head_col(st["sp_all"], h) * st["c0"][h] + head_col(st["sc_all"], h) * st["c_loc"][h]

    @pl.when(c == 0)
    def _():
        xbuf_ref[0:CONV_HALO, :] = jnp.zeros((CONV_HALO, xbuf_ref.shape[1]), F32)
        c_ref[...] = jnp.zeros(c_ref.shape, F32)
        m_ref[...] = jnp.zeros(m_ref.shape, F32)
        st = {}
        mlstm_prepare(st)
        strict = strict_mask()
        z_all = {h: scores_z(h, keys(h, 0, 1)) for h in heads}
        mlstm_weights(st)
        d_all, nlk_all = {}, {}
        for h in heads:
            nlk = softplus2(z_all[h])
            d_all[h] = z_all[h] - nlk
            nlk_all[h] = jnp.where(strict, nlk, 0.0)
        cs = {h: suffix(nlk_all[h]) for h in heads}
        mlstm_numerators(st)
        a_all = {h: jnp.where(strict, jnp.exp2(d_all[h] + cs[h]), 0.0).astype(BF16) for h in heads}
        acc0 = {h: jnp.dot(a_all[h], values(h, 0, 1), preferred_element_type=F32) for h in heads}
        mlstm_outputs(st)
        for h in heads:
            write_out(h, acc0[h])

    @pl.when(c > 0)
    def _():
        st = {}
        mlstm_prepare(st)
        strict = strict_mask()
        alive = [None] * N_HEADS
        def sb_scores(hs, sb):
            sb["z"] = {h: scores_z(h, keys(h, c - 1, 2)) for h in hs}

        def sb_suffix(hs, sb):
            sb["d"], sb["nd"], sb["np"] = {}, {}, {}
            for h in hs:
                nlk = softplus2(sb["z"][h])
                sb["d"][h] = sb["z"][h] - nlk
                sb["nd"][h] = jnp.where(strict, nlk[:, tq:], 0.0)
                sb["np"][h] = nlk[:, :tq]
            sb["cd"] = {h: suffix(sb["nd"][h]) for h in hs}
            sb["cp"] = {h: suffix(sb["np"][h]) for h in hs}

        def sb_values(hs, sb):
            carry_d = {h: total(sb["nd"][h]) for h in hs}
            a_all = {}
            for h in hs:
                a_d = jnp.where(strict, jnp.exp2(sb["d"][h][:, tq:] + sb["cd"][h]), 0.0)
                a_p = jnp.exp2(sb["d"][h][:, :tq] + sb["cp"][h] + carry_d[h])
                a_all[h] = jnp.concatenate([a_p, a_d], axis=1).astype(BF16)
            pv = {h: jnp.dot(a_all[h], values(h, c - 1, 2), preferred_element_type=F32) for h in hs}
            for h in hs:
                acc_ref[h] = pv[h]
                sb.setdefault("pv", {})[h] = pv[h]
                carry = carry_d[h] + total(sb["np"][h])
                carry_ref[h] = jnp.broadcast_to(carry, (tq, LANES))
                alive[h] = (jnp.max(carry) > UNDERFLOW_LOG2).astype(jnp.int32)

        groups = [range(g, g + SB_GROUP) for g in range(0, N_HEADS, SB_GROUP)]
        assert len(groups) == 4
        sbs = [{} for _ in groups]
        S, X, V = sb_scores, sb_suffix, sb_values
        W, N, O = mlstm_weights, mlstm_numerators, mlstm_outputs
        order = [(S, 0), (W,), (X, 0), (S, 1), (V, 0), (X, 1), (S, 2), (N,),
                 (V, 1), (X, 2), (S, 3), (V, 2), (X, 3), (O,), (V, 3)]
        for fn, *group in order:
            if group:
                fn(groups[group[0]], sbs[group[0]])
            else:
                fn(st)
        for hs, sb in zip(groups, sbs):
            for h in hs:
                write_out(h, sb["pv"][h])

        def cond(state):
            j, alive_h = state
            return jnp.logical_and(j >= 0, alive_h > 0)

        for h in heads:
            def body(state, h=h):
                j, _ = state
                z = scores_z(h, keys(h, j, 1))
                nlk = softplus2(z)
                carry = carry_ref[h][:, 0:1]
                a = jnp.exp2(z - nlk + suffix(nlk) + carry)
                acc = acc_ref[h] + jnp.dot(a.astype(BF16), values(h, j, 1), preferred_element_type=F32)
                acc_ref[h] = acc
                write_out(h, acc)
                carry = carry + total(nlk)
                carry_ref[h] = jnp.broadcast_to(carry, (tq, LANES))
                return j - 1, (jnp.max(carry) > UNDERFLOW_LOG2).astype(jnp.int32)

            lax.while_loop(cond, body, (c - 2, alive[h]))


def _mixers(p, gate_a, gates, gates_t, conv_w, bif_row, bif_col, norm_g, csm, B, S, L):
    T = B * S
    D = N_HEADS * HEAD_DIM
    nc = S // L
    tok = lambda blk: pl.BlockSpec((L, D), lambda b, c: (b * nc + c, blk))
    seq = lambda blk: pl.BlockSpec((S, D), lambda b, c: (b, blk), pipeline_mode=pl.Buffered(1))
    const = lambda shape: pl.BlockSpec(shape, lambda b, c: (0,) * len(shape))
    return pl.pallas_call(
        functools.partial(_mixers_kernel, L=L),
        grid=(B, nc),
        in_specs=[
            tok(PW_QK_A), tok(PW_V_A), tok(0),
            pl.BlockSpec((L, LANES), lambda b, c: (b * nc + c, 0)),
            pl.BlockSpec((1, 2 * N_HEADS, L), lambda b, c: (b * nc + c, 0, 0)),
            tok(PW_Q_B), seq(PW_K_B), seq(PW_V_B), tok(PW_GATE_B),
            const((CONV_K, D)), const((1, LANES)), const((2 * N_HEADS, 1)), const((1, D)),
            const((L, L)),
        ],
        out_specs=pl.BlockSpec((L, D), lambda b, c: (b * nc + c, 0)),
        out_shape=jax.ShapeDtypeStruct((T, D), BF16),
        scratch_shapes=[
            pltpu.VMEM((L + CONV_HALO, D), F32),
            pltpu.VMEM((N_HEADS, LANES, 2 * HEAD_DIM), F32),
            pltpu.VMEM((1, LANES), F32),
            pltpu.VMEM((L, D), F32),
            pltpu.VMEM((N_HEADS, L, HEAD_DIM), F32),
            pltpu.VMEM((N_HEADS, L, LANES), F32),
        ],
        compiler_params=pltpu.CompilerParams(
            dimension_semantics=("parallel", "arbitrary"), vmem_limit_bytes=VMEM_LIMIT),
        name="mixers",
    )(p, p, gate_a, gates, gates_t, p, p, p, p, conv_w, bif_row, bif_col, norm_g, csm)


def _mlp_kernel(x_ref, y_ref, wo_ref, g_ref, wu_ref, wd_ref, out_ref, *, tf):
    x1 = x_ref[...] + jnp.dot(y_ref[...], wo_ref[...], preferred_element_type=F32)
    ms = jnp.mean(x1 * x1, axis=-1, keepdims=True)
    h = (x1 * lax.rsqrt(ms + EPS) * g_ref[...]).astype(BF16)
    out_ref[...] = x1
    for f in range(wu_ref.shape[1] // tf):
        u = jnp.maximum(jnp.dot(h, wu_ref[:, f * tf:(f + 1) * tf], preferred_element_type=F32), 0.0)
        out_ref[...] += jnp.dot((u * u).astype(BF16), wd_ref[f * tf:(f + 1) * tf, :], preferred_element_type=F32)


def _outproj_mlp(x2d, y, w_out, g, w_up, w_down, tm, tf):
    T, D = x2d.shape
    F = w_up.shape[1]
    resident = lambda shape: pl.BlockSpec(shape, lambda i: (0,) * len(shape), pipeline_mode=pl.Buffered(1))
    return pl.pallas_call(
        functools.partial(_mlp_kernel, tf=tf),
        grid=(T // tm,),
        in_specs=[
            pl.BlockSpec((tm, D), lambda i: (i, 0)),
            pl.BlockSpec((tm, D), lambda i: (i, 0)),
            resident((D, D)),
            pl.BlockSpec((1, D), lambda i: (0, 0)),
            resident((D, F)),
            resident((F, D)),
        ],
        out_specs=pl.BlockSpec((tm, D), lambda i: (i, 0)),
        out_shape=jax.ShapeDtypeStruct((T, D), F32),
        compiler_params=pltpu.CompilerParams(
            dimension_semantics=("parallel",), vmem_limit_bytes=VMEM_LIMIT),
        name="outproj_mlp",
    )(x2d, y, w_out, g, w_up, w_down)


def _tiles(B, S):
    T = B * S
    return dict(
        tm_in=min(512, T),
        chunk=min(256, S),
        tm_mlp=min(1024, T),
        tf=1024,
    )


def kernel(x, norm_mix_g, w_in, b_if, b_gate, conv_w, mlstm_norm_g, sb_q_norm_g, sb_k_norm_g,
           w_out, norm_mlp_g, w_up, w_down):
    B, S, D = x.shape
    depth = w_in.shape[0]
    T = B * S
    assert D == N_HEADS * HEAD_DIM and w_in.shape[2] == 8 * D + 2 * N_HEADS
    cfg = _tiles(B, S)
    tq = cfg["chunk"]

    r = lax.broadcasted_iota(jnp.int32, (tq, tq), 0)
    c = lax.broadcasted_iota(jnp.int32, (tq, tq), 1)
    csm = jnp.where(r > c, -1.0, 0.0).astype(BF16)

    w_main_all, w_gate_all = _wprep(jnp.swapaxes(w_in, 1, 2))

    x2d = x.reshape(T, D)
    for l in range(depth):
        qk_gain = jnp.concatenate([jnp.tile(sb_q_norm_g[l], N_HEADS),
                                   jnp.tile(sb_k_norm_g[l] * (HEAD_DIM ** -0.5 * LOG2_E), N_HEADS)]
                                  ).reshape(1, 2 * D)
        p, gate_a, gates, gates_t = _inproj(
            x2d, norm_mix_g[l].reshape(1, D), w_main_all, w_gate_all, l, qk_gain,
            b_gate[l][:D].reshape(1, D), b_gate[l][D:].reshape(1, D), cfg["tm_in"])
        bif_row = jnp.pad(b_if[l], (0, LANES - 2 * N_HEADS)).reshape(1, LANES)
        bif_col = b_if[l].reshape(2 * N_HEADS, 1)
        L = cfg["chunk"]
        gates_t = gates_t.reshape(2 * N_HEADS, T // L, L).transpose(1, 0, 2)
        y = _mixers(p, gate_a, gates, gates_t, conv_w[l], bif_row, bif_col, mlstm_norm_g[l].reshape(1, D),
                    csm, B, S, L)
        x2d = _outproj_mlp(x2d, y, w_out[l].astype(BF16), norm_mlp_g[l].reshape(1, D),
                           w_up[l].astype(BF16), w_down[l].astype(BF16), cfg["tm_mlp"], cfg["tf"])
    return x2d.reshape(B, S, D)
```

```python
import functools

import jax
import jax.numpy as jnp
from jax import lax
from jax.experimental import pallas as pl
from jax.experimental.pallas import tpu as pltpu

F32 = jnp.float32
BF16 = jnp.bfloat16

EPS = 1e-6
N_HEADS = 8
HEAD_DIM = 128
DQK_A = 64
CONV_K = 4
LANES = 128
CONV_HALO = 8
VMEM_LIMIT = 56 * 1024 * 1024
VMEM_LIMIT_MIXERS = 58 * 1024 * 1024
LOG2_E = 1.4426950408889634
UNDERFLOW_LOG2 = -110.0 * LOG2_E

W_IN_BLOCKS = (0, 1, 2, 6, 7, 5, 3, 4)
N_BLK = 8
INPROJ_TN = 2
SB_GROUP = 2
PW_QK_A, PW_V_A, PW_GATE_B, PW_V_B, PW_Q_B, PW_K_B = range(6)


def _log_sigmoid(x):
    return jnp.minimum(x, 0.0) - jnp.log(1.0 + jnp.exp2(jnp.abs(x) * -LOG2_E))


def _exp_neg(x):
    return jnp.exp2(x * -LOG2_E)


def _sigmoid(x):
    return 1.0 / (1.0 + _exp_neg(x))


def _wprep_kernel(w_ref, wg_ref, main_ref, gate_ref):
    main_ref[0] = w_ref[0].astype(main_ref.dtype)
    g = wg_ref[0]
    row = lax.broadcasted_iota(jnp.int32, g.shape, 0)
    gate_ref[0] = jnp.where(row < 2 * N_HEADS, g, 0.0).astype(gate_ref.dtype)


def _wprep(w_in_t):
    depth, n_in, D = w_in_t.shape
    n_gate = 2 * N_HEADS
    gate_lo = 3 * D

    def block_row(b):
        pos = functools.reduce(lambda acc, kv: jnp.where(b == kv[0], kv[1], acc), enumerate(W_IN_BLOCKS), 0)
        return pl.multiple_of(pos * D + jnp.where(pos >= 3, n_gate, 0), n_gate)

    return pl.pallas_call(
        _wprep_kernel,
        grid=(depth, N_BLK),
        in_specs=[pl.BlockSpec((pl.Element(1), pl.Element(D), pl.Element(D)), lambda l, b: (l, block_row(b), 0)),
                  pl.BlockSpec((pl.Element(1), pl.Element(LANES), pl.Element(D)), lambda l, b: (l, gate_lo, 0))],
        out_specs=[pl.BlockSpec((1, D, D), lambda l, b: (l, b, 0)),
                   pl.BlockSpec((1, LANES, D), lambda l, b: (l, 0, 0))],
        out_shape=[jax.ShapeDtypeStruct((depth, N_BLK * D, D), BF16),
                   jax.ShapeDtypeStruct((depth, LANES, D), BF16)],
        compiler_params=pltpu.CompilerParams(
            dimension_semantics=("parallel", "arbitrary"), vmem_limit_bytes=VMEM_LIMIT),
        name="inproj_weights",
    )(w_in_t, w_in_t)


def _dot_nt(a, b_t):
    return lax.dot_general(a, b_t, (((1,), (1,)), ((), ())), preferred_element_type=F32)


def _inproj_kernel(x_ref, g_ref, w_ref, wg_ref, qkg_ref, bga_ref, bgb_ref,
                   p_ref, ga_ref, gates_ref, gates_t_ref):
    D = x_ref.shape[1]
    tn = INPROJ_TN * D
    x = x_ref[...]
    ms = jnp.mean(x * x, axis=-1, keepdims=True)
    h = (x * lax.rsqrt(ms + EPS) * g_ref[...]).astype(BF16)
    gates = _dot_nt(h, wg_ref[0])
    gates_ref[...] = gates
    gates_t_ref[...] = gates.T[:2 * N_HEADS, :]

    acc = _dot_nt(h, w_ref[0, 3 * tn:4 * tn, :])
    gain = qkg_ref[...]
    for hd in range(2 * N_HEADS):
        sl = slice(hd * HEAD_DIM, (hd + 1) * HEAD_DIM)
        a = acc[:, sl]
        ms = jnp.mean(a * a, axis=-1, keepdims=True)
        p_ref[:, 2 * tn + hd * HEAD_DIM:2 * tn + (hd + 1) * HEAD_DIM] = (
            a * lax.rsqrt(ms + EPS) * gain[:, sl]).astype(p_ref.dtype)

    acc = _dot_nt(h, w_ref[0, 1 * tn:2 * tn, :])
    gate = 1.0 / ((1.0 + _exp_neg(acc[:, :D])) * (1.0 + _exp_neg(acc[:, D:] + bga_ref[...])))
    ga_ref[...] = gate.astype(ga_ref.dtype)

    acc = _dot_nt(h, w_ref[0, 2 * tn:3 * tn, :])
    p_ref[:, tn:tn + D] = _sigmoid(acc[:, :D] + bgb_ref[...]).astype(p_ref.dtype)
    p_ref[:, tn + D:2 * tn] = acc[:, D:].astype(p_ref.dtype)

    acc = _dot_nt(h, w_ref[0, 0 * tn:1 * tn, :])
    p_ref[:, 0:tn] = acc.astype(p_ref.dtype)


def _inproj(x2d, g, w_main, w_gate, layer, qk_gain, b_gate_a, b_gate_b, tm):
    T, D = x2d.shape
    tn = INPROJ_TN * D
    n_rows = w_main.shape[1]
    assert n_rows == 4 * tn and qk_gain.shape == (1, tn)
    const = lambda shape: pl.BlockSpec(shape, lambda i: (0,) * len(shape))
    return pl.pallas_call(
        _inproj_kernel,
        grid=(T // tm,),
        in_specs=[
            pl.BlockSpec((tm, D), lambda i: (i, 0)),
            const((1, D)),
            pl.BlockSpec((1, n_rows, D), lambda i: (layer, 0, 0), pipeline_mode=pl.Buffered(1)),
            pl.BlockSpec((1, LANES, D), lambda i: (layer, 0, 0), pipeline_mode=pl.Buffered(1)),
            const((1, tn)), const((1, D)), const((1, D)),
        ],
        out_specs=[
            pl.BlockSpec((tm, 3 * tn), lambda i: (i, 0)),
            pl.BlockSpec((tm, D), lambda i: (i, 0)),
            pl.BlockSpec((tm, LANES), lambda i: (i, 0)),
            pl.BlockSpec((2 * N_HEADS, tm), lambda i: (0, i)),
        ],
        out_shape=[
            jax.ShapeDtypeStruct((T, 3 * tn), BF16),
            jax.ShapeDtypeStruct((T, D), BF16),
            jax.ShapeDtypeStruct((T, LANES), F32),
            jax.ShapeDtypeStruct((2 * N_HEADS, T), F32),
        ],
        compiler_params=pltpu.CompilerParams(
            dimension_semantics=("parallel",), vmem_limit_bytes=VMEM_LIMIT),
        name="norm_inproj",
    )(x2d, g, w_main, w_gate, qk_gain, b_gate_a, b_gate_b)


def _mixers_kernel(qk_ref, va_ref, ga_ref, gates_ref, gates_t_ref, q_ref, k_ref, v_ref, gb_ref,
                   convw_ref, bif_row_ref, bif_col_ref, ng_ref, csm_ref,
                   out_ref, xbuf_ref, c_ref, m_ref, ya_ref, acc_ref, carry_ref, *, L):
    c = pl.program_id(1)
    d_qk = N_HEADS * DQK_A
    heads = range(N_HEADS)
    sl = [slice(h * HEAD_DIM, (h + 1) * HEAD_DIM) for h in heads]
    tq = L
    csm = csm_ref[...]

    def keys(h, j, n):
        return k_ref[pl.ds(pl.multiple_of(j * tq, tq), n * tq), sl[h]]

    def values(h, j, n):
        return v_ref[pl.ds(pl.multiple_of(j * tq, tq), n * tq), sl[h]]

    def scores_z(h, kj):
        return lax.dot_general(q_ref[:, sl[h]], kj, (((1,), (1,)), ((), ())), preferred_element_type=F32)

    def softplus2(z):
        return jnp.maximum(z, 0.0) + jnp.log2(1.0 + jnp.exp2(-jnp.abs(z)))

    def suffix(nlk):
        return jnp.dot(nlk.astype(BF16), csm, preferred_element_type=F32)

    def total(nlk):
        return -jnp.sum(nlk, axis=1, keepdims=True)

    def strict_mask():
        row = lax.broadcasted_iota(jnp.int32, (tq, tq), 0)
        col = lax.broadcasted_iota(jnp.int32, (tq, tq), 1)
        return col < row

    def head_col(x, h):
        return x[:, N_HEADS + h:N_HEADS + h + 1]

    def write_out(h, acc):
        out_ref[:, sl[h]] = (ya_ref[:, sl[h]] + gb_ref[:, sl[h]].astype(F32) * acc).astype(out_ref.dtype)

    def mlstm_prepare(st):
        xbuf_ref[CONV_HALO:CONV_HALO + L, :] = qk_ref[...].astype(F32)
        w = convw_ref[...]
        y = w[CONV_K - 1:CONV_K, :] * xbuf_ref[CONV_HALO:CONV_HALO + L, :]
        for tap in range(CONV_K - 1):
            off = CONV_HALO - (CONV_K - 1) + tap
            y = y + w[tap:tap + 1, :] * xbuf_ref[off:off + L, :]
        xbuf_ref[0:CONV_HALO, :] = xbuf_ref[L:L + CONV_HALO, :]
        qk = y * _sigmoid(y)

        row = lax.broadcasted_iota(jnp.int32, (L, L), 0)
        col = lax.broadcasted_iota(jnp.int32, (L, L), 1)
        st["causal"] = causal = col <= row
        tri = jnp.where(causal, 1.0, 0.0).astype(F32)
        pre_col = gates_ref[...] + bif_row_ref[...]
        b_col_all = jnp.dot(tri, _log_sigmoid(pre_col), preferred_element_type=F32,
                            precision=lax.Precision.HIGHEST)
        st["pre_row"] = pre_row = gates_t_ref[0] + bif_col_ref[...]
        st["b_row_all"] = lax.dot_general(_log_sigmoid(pre_row), tri, (((1,), (1,)), ((), ())),
                                          preferred_element_type=F32, precision=lax.Precision.HIGHEST)

        lane = lax.broadcasted_iota(jnp.int32, (L, LANES), 1)
        head_lanes = jnp.logical_and(lane >= N_HEADS, lane < 2 * N_HEADS)
        b_all = jnp.where(head_lanes, b_col_all, 0.0)
        i_all = jnp.where(head_lanes, pltpu.roll(pre_col, N_HEADS, axis=1), 0.0)
        rowi = lax.broadcasted_iota(jnp.int32, (L, LANES), 0)
        g_max = i_all - b_all
        shift = 1
        while shift < L:
            g_max = jnp.maximum(g_max, jnp.where(rowi >= shift, pltpu.roll(g_max, shift, axis=0), -jnp.inf))
            shift *= 2
        m_prev = m_ref[...]
        st["m_rel"] = m_rel = jnp.maximum(m_prev, g_max)
        st["inter_all"] = jnp.exp(m_prev - m_rel)
        st["floor_all"] = jnp.exp(-(b_all + m_rel))
        b_last = b_all[L - 1:L, :]
        a_all = b_last - b_all + i_all
        m_loc = jnp.max(a_all, axis=0, keepdims=True)
        w_all = jnp.exp(a_all - m_loc)
        m_new = jnp.maximum(b_last + m_prev, m_loc)
        st["sp_all"] = jnp.exp(b_last + m_prev - m_new)
        st["sc_all"] = jnp.exp(m_loc - m_new)
        m_ref[...] = m_new

        q_pair, km_b, vh, c0, s_qk, q_c, c_loc = ({} for _ in range(7))
        for h in heads:
            pair, half = divmod(h, 2)
            q_pair[h] = qk[:, pair * LANES:(pair + 1) * LANES].astype(BF16)
            k_pair = qk[:, d_qk + pair * LANES:d_qk + (pair + 1) * LANES]
            in_head = jnp.logical_and(lane >= half * DQK_A, lane < (half + 1) * DQK_A)
            km_b[h] = jnp.where(in_head, k_pair * (DQK_A ** -0.5), 0.0).astype(BF16)
            vh[h] = va_ref[:, sl[h]]
            c0[h] = c_ref[h]
        for h in heads:
            s_qk[h] = lax.dot_general(q_pair[h], km_b[h], (((1,), (1,)), ((), ())), preferred_element_type=F32)
        for h in heads:
            q_c[h] = jnp.dot(q_pair[h], c0[h].astype(BF16), preferred_element_type=F32)
        for h in heads:
            w_t = head_col(w_all, h)
            wv = jnp.concatenate([w_t * vh[h].astype(F32), jnp.broadcast_to(w_t, (L, HEAD_DIM))],
                                 axis=1).astype(BF16)
            c_loc[h] = lax.dot_general(km_b[h], wv, (((0,), (0,)), ((), ())), preferred_element_type=F32)
        st.update(vh=vh, c0=c0, s_qk=s_qk, q_c=q_c, c_loc=c_loc)

    def mlstm_weights(st):
        p = {}
        for h in heads:
            g_s = st["pre_row"][h:h + 1, :] - st["b_row_all"][N_HEADS + h:N_HEADS + h + 1, :]
            p[h] = jnp.exp(jnp.where(st["causal"], g_s - head_col(st["m_rel"], h), -jnp.inf)) * st["s_qk"][h]
        st["p"] = p

    def mlstm_numerators(st):
        ones_cols = jnp.ones((L, HEAD_DIM), BF16)
        st["p_v"] = {h: jnp.dot(st["p"][h].astype(BF16), jnp.concatenate([st["vh"][h], ones_cols], axis=1),
                                preferred_element_type=F32) for h in heads}

    def mlstm_outputs(st):
        mean_cols = jnp.full((HEAD_DIM, HEAD_DIM), 1.0 / HEAD_DIM, BF16)
        for h in heads:
            both = st["p_v"][h] + head_col(st["inter_all"], h) * st["q_c"][h]
            num, den = both[:, :HEAD_DIM], both[:, HEAD_DIM:]
            hh = num * (1.0 / jnp.maximum(jnp.abs(den), head_col(st["floor_all"], h)))
            ms = jnp.dot((hh * hh).astype(BF16), mean_cols, preferred_element_type=F32)
            hn = hh * lax.rsqrt(ms + EPS) * ng_ref[:, sl[h]]
            ya_ref[:, sl[h]] = ga_ref[:, sl[h]].astype(F32) * hn
        for h in heads:
            c_ref[h] = head_col(st["sp_all"], h) * st["c0"][h] + head_col(st["sc_all"], h) * st["c_loc"][h]

    @pl.when(c == 0)
    def _():
        xbuf_ref[0:CONV_HALO, :] = jnp.zeros((CONV_HALO, xbuf_ref.shape[1]), F32)
        c_ref[...] = jnp.zeros(c_ref.shape, F32)
        m_ref[...] = jnp.zeros(m_ref.shape, F32)
        st = {}
        mlstm_prepare(st)
        strict = strict_mask()
        z_all = {h: scores_z(h, keys(h, 0, 1)) for h in heads}
        mlstm_weights(st)
        d_all, nlk_all = {}, {}
        for h in heads:
            nlk = softplus2(z_all[h])
            d_all[h] = z_all[h] - nlk
            nlk_all[h] = jnp.where(strict, nlk, 0.0)
        cs = {h: suffix(nlk_all[h]) for h in heads}
        mlstm_numerators(st)
        a_all = {h: jnp.where(strict, jnp.exp2(d_all[h] + cs[h]), 0.0).astype(BF16) for h in heads}
        acc0 = {h: jnp.dot(a_all[h], values(h, 0, 1), preferred_element_type=F32) for h in heads}
        mlstm_outputs(st)
        for h in heads:
            write_out(h, acc0[h])

    @pl.when(c > 0)
    def _():
        st = {}
        mlstm_prepare(st)
        strict = strict_mask()
        alive = [None] * N_HEADS
        def sb_scores(hs, sb):
            sb["z"] = {h: scores_z(h, keys(h, c - 1, 2)) for h in hs}

        def sb_suffix(hs, sb):
            sb["d"], sb["nd"], sb["np"] = {}, {}, {}
            for h in hs:
                nlk = softplus2(sb["z"][h])
                sb["d"][h] = sb["z"][h] - nlk
                sb["nd"][h] = jnp.where(strict, nlk[:, tq:], 0.0)
                sb["np"][h] = nlk[:, :tq]
            sb["cd"] = {h: suffix(sb["nd"][h]) for h in hs}
            sb["cp"] = {h: suffix(sb["np"][h]) for h in hs}

        def sb_values(hs, sb):
            carry_d = {h: total(sb["nd"][h]) for h in hs}
            a_all = {}
            for h in hs:
                a_d = jnp.where(strict, jnp.exp2(sb["d"][h][:, tq:] + sb["cd"][h]), 0.0)
                a_p = jnp.exp2(sb["d"][h][:, :tq] + sb["cp"][h] + carry_d[h])
                a_all[h] = jnp.concatenate([a_p, a_d], axis=1).astype(BF16)
            pv = {h: jnp.dot(a_all[h], values(h, c - 1, 2), preferred_element_type=F32) for h in hs}
            for h in hs:
                acc_ref[h] = pv[h]
                sb.setdefault("pv", {})[h] = pv[h]
                carry = carry_d[h] + total(sb["np"][h])
                carry_ref[h] = jnp.broadcast_to(carry, (tq, LANES))
                alive[h] = (jnp.max(carry) > UNDERFLOW_LOG2).astype(jnp.int32)

        groups = [range(g, g + SB_GROUP) for g in range(0, N_HEADS, SB_GROUP)]
        assert len(groups) == 4
        sbs = [{} for _ in groups]
        S, X, V = sb_scores, sb_suffix, sb_values
        W, N, O = mlstm_weights, mlstm_numerators, mlstm_outputs
        order = [(S, 0), (W,), (X, 0), (S, 1), (V, 0), (X, 1), (S, 2), (N,),
                 (V, 1), (X, 2), (S, 3), (V, 2), (X, 3), (O,), (V, 3)]
        for fn, *group in order:
            if group:
                fn(groups[group[0]], sbs[group[0]])
            else:
                fn(st)
        for hs, sb in zip(groups, sbs):
            for h in hs:
                write_out(h, sb["pv"][h])

        def cond(state):
            j, alive_h = state
            return jnp.logical_and(j >= 0, alive_h > 0)

        for h in heads:
            def body(state, h=h):
                j, _ = state
                z = scores_z(h, keys(h, j, 1))
                nlk = softplus2(z)
                carry = carry_ref[h][:, 0:1]
                a = jnp.exp2(z - nlk + suffix(nlk) + carry)
                acc = acc_ref[h] + jnp.dot(a.astype(BF16), values(h, j, 1), preferred_element_type=F32)
                acc_ref[h] = acc
                write_out(h, acc)
                carry = carry + total(nlk)
                carry_ref[h] = jnp.broadcast_to(carry, (tq, LANES))
                return j - 1, (jnp.max(carry) > UNDERFLOW_LOG2).astype(jnp.int32)

            lax.while_loop(cond, body, (c - 2, alive[h]))


def _mixers(p, gate_a, gates, gates_t, conv_w, bif_row, bif_col, norm_g, csm, B, S, L):
    T = B * S
    D = N_HEADS * HEAD_DIM
    nc = S // L
    tok = lambda blk: pl.BlockSpec((L, D), lambda b, c: (b * nc + c, blk))
    seq = lambda blk: pl.BlockSpec((S, D), lambda b, c: (b, blk))
    const = lambda shape: pl.BlockSpec(shape, lambda b, c: (0,) * len(shape))
    return pl.pallas_call(
        functools.partial(_mixers_kernel, L=L),
        grid=(B, nc),
        in_specs=[
            tok(PW_QK_A), tok(PW_V_A), tok(0),
            pl.BlockSpec((L, LANES), lambda b, c: (b * nc + c, 0)),
            pl.BlockSpec((1, 2 * N_HEADS, L), lambda b, c: (b * nc + c, 0, 0)),
            tok(PW_Q_B), seq(PW_K_B), seq(PW_V_B), tok(PW_GATE_B),
            const((CONV_K, D)), const((1, LANES)), const((2 * N_HEADS, 1)), const((1, D)),
            const((L, L)),
        ],
        out_specs=pl.BlockSpec((L, D), lambda b, c: (b * nc + c, 0)),
        out_shape=jax.ShapeDtypeStruct((T, D), BF16),
        scratch_shapes=[
            pltpu.VMEM((L + CONV_HALO, D), F32),
            pltpu.VMEM((N_HEADS, LANES, 2 * HEAD_DIM), F32),
            pltpu.VMEM((1, LANES), F32),
            pltpu.VMEM((L, D), F32),
            pltpu.VMEM((N_HEADS, L, HEAD_DIM), F32),
            pltpu.VMEM((N_HEADS, L, LANES), F32),
        ],
        compiler_params=pltpu.CompilerParams(
            dimension_semantics=("parallel", "arbitrary"), vmem_limit_bytes=VMEM_LIMIT_MIXERS),
        name="mixers",
    )(p, p, gate_a, gates, gates_t, p, p, p, p, conv_w, bif_row, bif_col, norm_g, csm)


def _mlp_kernel(x_ref, y_ref, wo_ref, g_ref, wu_ref, wd_ref, out_ref, *, tf):
    x1 = x_ref[...] + jnp.dot(y_ref[...], wo_ref[...], preferred_element_type=F32)
    ms = jnp.mean(x1 * x1, axis=-1, keepdims=True)
    h = (x1 * lax.rsqrt(ms + EPS) * g_ref[...]).astype(BF16)
    out_ref[...] = x1
    for f in range(wu_ref.shape[1] // tf):
        u = jnp.maximum(jnp.dot(h, wu_ref[:, f * tf:(f + 1) * tf], preferred_element_type=F32), 0.0)
        out_ref[...] += jnp.dot((u * u).astype(BF16), wd_ref[f * tf:(f + 1) * tf, :], preferred_element_type=F32)


def _outproj_mlp(x2d, y, w_out, g, w_up, w_down, tm, tf):
    T, D = x2d.shape
    F = w_up.shape[1]
    resident = lambda shape: pl.BlockSpec(shape, lambda i: (0,) * len(shape), pipeline_mode=pl.Buffered(1))
    return pl.pallas_call(
        functools.partial(_mlp_kernel, tf=tf),
        grid=(T // tm,),
        in_specs=[
            pl.BlockSpec((tm, D), lambda i: (i, 0)),
            pl.BlockSpec((tm, D), lambda i: (i, 0)),
            resident((D, D)),
            pl.BlockSpec((1, D), lambda i: (0, 0)),
            resident((D, F)),
            resident((F, D)),
        ],
        out_specs=pl.BlockSpec((tm, D), lambda i: (i, 0)),
        out_shape=jax.ShapeDtypeStruct((T, D), F32),
        compiler_params=pltpu.CompilerParams(
            dimension_semantics=("parallel",), vmem_limit_bytes=VMEM_LIMIT),
        name="outproj_mlp",
    )(x2d, y, w_out, g, w_up, w_down)


def _tiles(B, S):
    T = B * S
    return dict(
        tm_in=min(512, T),
        chunk=min(256, S),
        tm_mlp=min(1024, T),
        tf=1024,
    )


def kernel(x, norm_mix_g, w_in, b_if, b_gate, conv_w, mlstm_norm_g, sb_q_norm_g, sb_k_norm_g,
           w_out, norm_mlp_g, w_up, w_down):
    B, S, D = x.shape
    depth = w_in.shape[0]
    T = B * S
    assert D == N_HEADS * HEAD_DIM and w_in.shape[2] == 8 * D + 2 * N_HEADS
    cfg = _tiles(B, S)
    tq = cfg["chunk"]

    r = lax.broadcasted_iota(jnp.int32, (tq, tq), 0)
    c = lax.broadcasted_iota(jnp.int32, (tq, tq), 1)
    csm = jnp.where(r > c, -1.0, 0.0).astype(BF16)

    w_main_all, w_gate_all = _wprep(jnp.swapaxes(w_in, 1, 2))

    x2d = x.reshape(T, D)
    for l in range(depth):
        qk_gain = jnp.concatenate([jnp.tile(sb_q_norm_g[l], N_HEADS),
                                   jnp.tile(sb_k_norm_g[l] * (HEAD_DIM ** -0.5 * LOG2_E), N_HEADS)]
                                  ).reshape(1, 2 * D)
        p, gate_a, gates, gates_t = _inproj(
            x2d, norm_mix_g[l].reshape(1, D), w_main_all, w_gate_all, l, qk_gain,
            b_gate[l][:D].reshape(1, D), b_gate[l][D:].reshape(1, D), cfg["tm_in"])
        bif_row = jnp.pad(b_if[l], (0, LANES - 2 * N_HEADS)).reshape(1, LANES)
        bif_col = b_if[l].reshape(2 * N_HEADS, 1)
        L = cfg["chunk"]
        gates_t = gates_t.reshape(2 * N_HEADS, T // L, L).transpose(1, 0, 2)
        y = _mixers(p, gate_a, gates, gates_t, conv_w[l], bif_row, bif_col, mlstm_norm_g[l].reshape(1, D),
                    csm, B, S, L)
        x2d = _outproj_mlp(x2d, y, w_out[l].astype(BF16), norm_mlp_g[l].reshape(1, D),
                           w_up[l].astype(BF16), w_down[l].astype(BF16), cfg["tm_mlp"], cfg["tf"])
    return x2d.reshape(B, S, D)
```

```python
import functools

import jax
import jax.numpy as jnp
from jax import lax
from jax.experimental import pallas as pl
from jax.experimental.pallas import tpu as pltpu

F32 = jnp.float32
BF16 = jnp.bfloat16

EPS = 1e-6
N_HEADS = 8
HEAD_DIM = 128
DQK_A = 64
CONV_K = 4
LANES = 128
CONV_HALO = 8
VMEM_LIMIT = 56 * 1024 * 1024
VMEM_LIMIT_MIXERS = 58 * 1024 * 1024
LOG2_E = 1.4426950408889634
UNDERFLOW_LOG2 = -110.0 * LOG2_E

W_IN_BLOCKS = (0, 1, 2, 6, 7, 5, 3, 4)
N_BLK = 8
INPROJ_TN = 2
SB_GROUP = 2
PW_QK_A, PW_V_A, PW_GATE_B, PW_V_B, PW_Q_B, PW_K_B = range(6)


def _log_sigmoid(x):
    return jnp.minimum(x, 0.0) - jnp.log(1.0 + jnp.exp2(jnp.abs(x) * -LOG2_E))


def _exp_neg(x):
    return jnp.exp2(x * -LOG2_E)


def _sigmoid(x):
    return 1.0 / (1.0 + _exp_neg(x))


def _wprep_kernel(w_ref, wg_ref, main_ref, gate_ref):
    main_ref[0] = w_ref[0].astype(main_ref.dtype)
    g = wg_ref[0]
    row = lax.broadcasted_iota(jnp.int32, g.shape, 0)
    gate_ref[0] = jnp.where(row < 2 * N_HEADS, g, 0.0).astype(gate_ref.dtype)


def _wprep(w_in_t):
    depth, n_in, D = w_in_t.shape
    n_gate = 2 * N_HEADS
    gate_lo = 3 * D

    def block_row(b):
        pos = functools.reduce(lambda acc, kv: jnp.where(b == kv[0], kv[1], acc), enumerate(W_IN_BLOCKS), 0)
        return pl.multiple_of(pos * D + jnp.where(pos >= 3, n_gate, 0), n_gate)

    return pl.pallas_call(
        _wprep_kernel,
        grid=(depth, N_BLK),
        in_specs=[pl.BlockSpec((pl.Element(1), pl.Element(D), pl.Element(D)), lambda l, b: (l, block_row(b), 0)),
                  pl.BlockSpec((pl.Element(1), pl.Element(LANES), pl.Element(D)), lambda l, b: (l, gate_lo, 0))],
        out_specs=[pl.BlockSpec((1, D, D), lambda l, b: (l, b, 0)),
                   pl.BlockSpec((1, LANES, D), lambda l, b: (l, 0, 0))],
        out_shape=[jax.ShapeDtypeStruct((depth, N_BLK * D, D), BF16),
                   jax.ShapeDtypeStruct((depth, LANES, D), BF16)],
        compiler_params=pltpu.CompilerParams(
            dimension_semantics=("parallel", "arbitrary"), vmem_limit_bytes=VMEM_LIMIT),
        name="inproj_weights",
    )(w_in_t, w_in_t)


def _dot_nt(a, b_t):
    return lax.dot_general(a, b_t, (((1,), (1,)), ((), ())), preferred_element_type=F32)


def _inproj_kernel(x_ref, g_ref, w_ref, wg_ref, qkg_ref, bga_ref, bgb_ref,
                   p_ref, ga_ref, gates_ref, gates_t_ref, *, chunk):
    D = x_ref.shape[1]
    tn = INPROJ_TN * D
    x = x_ref[...]
    ms = jnp.mean(x * x, axis=-1, keepdims=True)
    h = (x * lax.rsqrt(ms + EPS) * g_ref[...]).astype(BF16)
    gates = _dot_nt(h, wg_ref[0])
    gates_ref[...] = gates
    gates_t = gates.T[:2 * N_HEADS, :]
    for k in range(gates_t_ref.shape[0]):
        gates_t_ref[k] = gates_t[:, k * chunk:(k + 1) * chunk]

    acc = _dot_nt(h, w_ref[0, 3 * tn:4 * tn, :])
    gain = qkg_ref[...]
    for hd in range(2 * N_HEADS):
        sl = slice(hd * HEAD_DIM, (hd + 1) * HEAD_DIM)
        a = acc[:, sl]
        ms = jnp.mean(a * a, axis=-1, keepdims=True)
        p_ref[:, 2 * tn + hd * HEAD_DIM:2 * tn + (hd + 1) * HEAD_DIM] = (
            a * lax.rsqrt(ms + EPS) * gain[:, sl]).astype(p_ref.dtype)

    acc = _dot_nt(h, w_ref[0, 1 * tn:2 * tn, :])
    gate = 1.0 / ((1.0 + _exp_neg(acc[:, :D])) * (1.0 + _exp_neg(acc[:, D:] + bga_ref[...])))
    ga_ref[...] = gate.astype(ga_ref.dtype)

    acc = _dot_nt(h, w_ref[0, 2 * tn:3 * tn, :])
    p_ref[:, tn:tn + D] = _sigmoid(acc[:, :D] + bgb_ref[...]).astype(p_ref.dtype)
    p_ref[:, tn + D:2 * tn] = acc[:, D:].astype(p_ref.dtype)

    acc = _dot_nt(h, w_ref[0, 0 * tn:1 * tn, :])
    p_ref[:, 0:tn] = acc.astype(p_ref.dtype)


def _inproj(x2d, g, w_main, w_gate, layer, qk_gain, b_gate_a, b_gate_b, tm, chunk):
    T, D = x2d.shape
    tn = INPROJ_TN * D
    n_rows = w_main.shape[1]
    assert n_rows == 4 * tn and qk_gain.shape == (1, tn) and tm % chunk == 0
    const = lambda shape: pl.BlockSpec(shape, lambda i: (0,) * len(shape))
    return pl.pallas_call(
        functools.partial(_inproj_kernel, chunk=chunk),
        grid=(T // tm,),
        in_specs=[
            pl.BlockSpec((tm, D), lambda i: (i, 0)),
            const((1, D)),
            pl.BlockSpec((1, n_rows, D), lambda i: (layer, 0, 0), pipeline_mode=pl.Buffered(1)),
            pl.BlockSpec((1, LANES, D), lambda i: (layer, 0, 0), pipeline_mode=pl.Buffered(1)),
            const((1, tn)), const((1, D)), const((1, D)),
        ],
        out_specs=[
            pl.BlockSpec((tm, 3 * tn), lambda i: (i, 0)),
            pl.BlockSpec((tm, D), lambda i: (i, 0)),
            pl.BlockSpec((tm, LANES), lambda i: (i, 0)),
            pl.BlockSpec((tm // chunk, 2 * N_HEADS, chunk), lambda i: (i, 0, 0)),
        ],
        out_shape=[
            jax.ShapeDtypeStruct((T, 3 * tn), BF16),
            jax.ShapeDtypeStruct((T, D), BF16),
            jax.ShapeDtypeStruct((T, LANES), F32),
            jax.ShapeDtypeStruct((T // chunk, 2 * N_HEADS, chunk), F32),
        ],
        compiler_params=pltpu.CompilerParams(
            dimension_semantics=("parallel",), vmem_limit_bytes=VMEM_LIMIT),
        name="norm_inproj",
    )(x2d, g, w_main, w_gate, qk_gain, b_gate_a, b_gate_b)


def _mixers_kernel(qk_ref, va_ref, ga_ref, gates_ref, gates_t_ref, q_ref, k_ref, v_ref, gb_ref,
                   convw_ref, bif_row_ref, bif_col_ref, ng_ref, csm_ref,
                   out_ref, xbuf_ref, c_ref, m_ref, ya_ref, acc_ref, carry_ref, *, L):
    c = pl.program_id(1)
    d_qk = N_HEADS * DQK_A
    heads = range(N_HEADS)
    sl = [slice(h * HEAD_DIM, (h + 1) * HEAD_DIM) for h in heads]
    tq = L
    csm = csm_ref[...]

    def keys(h, j, n):
        return k_ref[pl.ds(pl.multiple_of(j * tq, tq), n * tq), sl[h]]

    def values(h, j, n):
        return v_ref[pl.ds(pl.multiple_of(j * tq, tq), n * tq), sl[h]]

    def scores_z(h, kj):
        return lax.dot_general(q_ref[:, sl[h]], kj, (((1,), (1,)), ((), ())), preferred_element_type=F32)

    def softplus2(z):
        return jnp.maximum(z, 0.0) + jnp.log2(1.0 + jnp.exp2(-jnp.abs(z)))

    def suffix(nlk):
        return jnp.dot(nlk.astype(BF16), csm, preferred_element_type=F32)

    def total(nlk):
        return -jnp.sum(nlk, axis=1, keepdims=True)

    def strict_mask():
        row = lax.broadcasted_iota(jnp.int32, (tq, tq), 0)
        col = lax.broadcasted_iota(jnp.int32, (tq, tq), 1)
        return col < row

    def head_col(x, h):
        return x[:, N_HEADS + h:N_HEADS + h + 1]

    def write_out(h, acc):
        out_ref[:, sl[h]] = (ya_ref[:, sl[h]] + gb_ref[:, sl[h]].astype(F32) * acc).astype(out_ref.dtype)

    def mlstm_prepare(st):
        xbuf_ref[CONV_HALO:CONV_HALO + L, :] = qk_ref[...].astype(F32)
        w = convw_ref[...]
        y = w[CONV_K - 1:CONV_K, :] * xbuf_ref[CONV_HALO:CONV_HALO + L, :]
        for tap in range(CONV_K - 1):
            off = CONV_HALO - (CONV_K - 1) + tap
            y = y + w[tap:tap + 1, :] * xbuf_ref[off:off + L, :]
        xbuf_ref[0:CONV_HALO, :] = xbuf_ref[L:L + CONV_HALO, :]
        qk = y * _sigmoid(y)

        row = lax.broadcasted_iota(jnp.int32, (L, L), 0)
        col = lax.broadcasted_iota(jnp.int32, (L, L), 1)
        st["causal"] = causal = col <= row
        tri = jnp.where(causal, 1.0, 0.0).astype(F32)
        pre_col = gates_ref[...] + bif_row_ref[...]
        b_col_all = jnp.dot(tri, _log_sigmoid(pre_col), preferred_element_type=F32,
                            precision=lax.Precision.HIGHEST)
        st["pre_row"] = pre_row = gates_t_ref[0] + bif_col_ref[...]
        st["b_row_all"] = lax.dot_general(_log_sigmoid(pre_row), tri, (((1,), (1,)), ((), ())),
                                          preferred_element_type=F32, precision=lax.Precision.HIGHEST)

        lane = lax.broadcasted_iota(jnp.int32, (L, LANES), 1)
        head_lanes = jnp.logical_and(lane >= N_HEADS, lane < 2 * N_HEADS)
        b_all = jnp.where(head_lanes, b_col_all, 0.0)
        i_all = jnp.where(head_lanes, pltpu.roll(pre_col, N_HEADS, axis=1), 0.0)
        rowi = lax.broadcasted_iota(jnp.int32, (L, LANES), 0)
        g_max = i_all - b_all
        shift = 1
        while shift < L:
            g_max = jnp.maximum(g_max, jnp.where(rowi >= shift, pltpu.roll(g_max, shift, axis=0), -jnp.inf))
            shift *= 2
        m_prev = m_ref[...]
        st["m_rel"] = m_rel = jnp.maximum(m_prev, g_max)
        st["inter_all"] = jnp.exp(m_prev - m_rel)
        st["floor_all"] = jnp.exp(-(b_all + m_rel))
        b_last = b_all[L - 1:L, :]
        a_all = b_last - b_all + i_all
        m_loc = jnp.max(a_all, axis=0, keepdims=True)
        w_all = jnp.exp(a_all - m_loc)
        m_new = jnp.maximum(b_last + m_prev, m_loc)
        st["sp_all"] = jnp.exp(b_last + m_prev - m_new)
        st["sc_all"] = jnp.exp(m_loc - m_new)
        m_ref[...] = m_new

        q_pair, km_b, vh, c0, s_qk, q_c, c_loc = ({} for _ in range(7))
        for h in heads:
            pair, half = divmod(h, 2)
            q_pair[h] = qk[:, pair * LANES:(pair + 1) * LANES].astype(BF16)
            k_pair = qk[:, d_qk + pair * LANES:d_qk + (pair + 1) * LANES]
            in_head = jnp.logical_and(lane >= half * DQK_A, lane < (half + 1) * DQK_A)
            km_b[h] = jnp.where(in_head, k_pair * (DQK_A ** -0.5), 0.0).astype(BF16)
            vh[h] = va_ref[:, sl[h]]
            c0[h] = c_ref[h]
        for h in heads:
            s_qk[h] = lax.dot_general(q_pair[h], km_b[h], (((1,), (1,)), ((), ())), preferred_element_type=F32)
        for h in heads:
            q_c[h] = jnp.dot(q_pair[h], c0[h].astype(BF16), preferred_element_type=F32)
        for h in heads:
            w_t = head_col(w_all, h)
            wv = jnp.concatenate([w_t * vh[h].astype(F32), jnp.broadcast_to(w_t, (L, HEAD_DIM))],
                                 axis=1).astype(BF16)
            c_loc[h] = lax.dot_general(km_b[h], wv, (((0,), (0,)), ((), ())), preferred_element_type=F32)
        st.update(vh=vh, c0=c0, s_qk=s_qk, q_c=q_c, c_loc=c_loc)

    def mlstm_weights(st):
        p = {}
        for h in heads:
            g_s = st["pre_row"][h:h + 1, :] - st["b_row_all"][N_HEADS + h:N_HEADS + h + 1, :]
            p[h] = jnp.exp(jnp.where(st["causal"], g_s - head_col(st["m_rel"], h), -jnp.inf)) * st["s_qk"][h]
        st["p"] = p

    def mlstm_numerators(st):
        ones_cols = jnp.ones((L, HEAD_DIM), BF16)
        st["p_v"] = {h: jnp.dot(st["p"][h].astype(BF16), jnp.concatenate([st["vh"][h], ones_cols], axis=1),
                                preferred_element_type=F32) for h in heads}

    def mlstm_outputs(st):
        mean_cols = jnp.full((HEAD_DIM, HEAD_DIM), 1.0 / HEAD_DIM, BF16)
        for h in heads:
            both = st["p_v"][h] + head_col(st["inter_all"], h) * st["q_c"][h]
            num, den = both[:, :HEAD_DIM], both[:, HEAD_DIM:]
            hh = num * (1.0 / jnp.maximum(jnp.abs(den), head_col(st["floor_all"], h)))
            ms = jnp.dot((hh * hh).astype(BF16), mean_cols, preferred_element_type=F32)
            hn = hh * lax.rsqrt(ms + EPS) * ng_ref[:, sl[h]]
            ya_ref[:, sl[h]] = ga_ref[:, sl[h]].astype(F32) * hn
        for h in heads:
            c_ref[h] = head_col(st["sp_all"], h) * st["c0"][h] + head_col(st["sc_all"], h) * st["c_loc"][h]

    @pl.when(c == 0)
    def _():
        xbuf_ref[0:CONV_HALO, :] = jnp.zeros((CONV_HALO, xbuf_ref.shape[1]), F32)
        c_ref[...] = jnp.zeros(c_ref.shape, F32)
        m_ref[...] = jnp.zeros(m_ref.shape, F32)
        st = {}
        mlstm_prepare(st)
        strict = strict_mask()
        z_all = {h: scores_z(h, keys(h, 0, 1)) for h in heads}
        mlstm_weights(st)
        d_all, nlk_all = {}, {}
        for h in heads:
            nlk = softplus2(z_all[h])
            d_all[h] = z_all[h] - nlk
            nlk_all[h] = jnp.where(strict, nlk, 0.0)
        cs = {h: suffix(nlk_all[h]) for h in heads}
        mlstm_numerators(st)
        a_all = {h: jnp.where(strict, jnp.exp2(d_all[h] + cs[h]), 0.0).astype(BF16) for h in heads}
        acc0 = {h: jnp.dot(a_all[h], values(h, 0, 1), preferred_element_type=F32) for h in heads}
        mlstm_outputs(st)
        for h in heads:
            write_out(h, acc0[h])

    @pl.when(c > 0)
    def _():
        st = {}
        mlstm_prepare(st)
        strict = strict_mask()
        alive = [None] * N_HEADS
        def sb_scores(hs, sb):
            sb["z"] = {h: scores_z(h, keys(h, c - 1, 2)) for h in hs}

        def sb_suffix(hs, sb):
            sb["d"], sb["nd"], sb["np"] = {}, {}, {}
            for h in hs:
                nlk = softplus2(sb["z"][h])
                sb["d"][h] = sb["z"][h] - nlk
                sb["nd"][h] = jnp.where(strict, nlk[:, tq:], 0.0)
                sb["np"][h] = nlk[:, :tq]
            sb["cd"] = {h: suffix(sb["nd"][h]) for h in hs}
            sb["cp"] = {h: suffix(sb["np"][h]) for h in hs}

        def sb_values(hs, sb):
            carry_d = {h: total(sb["nd"][h]) for h in hs}
            a_all = {}
            for h in hs:
                a_d = jnp.where(strict, jnp.exp2(sb["d"][h][:, tq:] + sb["cd"][h]), 0.0)
                a_p = jnp.exp2(sb["d"][h][:, :tq] + sb["cp"][h] + carry_d[h])
                a_all[h] = jnp.concatenate([a_p, a_d], axis=1).astype(BF16)
            pv = {h: jnp.dot(a_all[h], values(h, c - 1, 2), preferred_element_type=F32) for h in hs}
            for h in hs:
                acc_ref[h] = pv[h]
                sb.setdefault("pv", {})[h] = pv[h]
                carry = carry_d[h] + total(sb["np"][h])
                carry_ref[h] = jnp.broadcast_to(carry, (tq, LANES))
                alive[h] = (jnp.max(carry) > UNDERFLOW_LOG2).astype(jnp.int32)

        groups = [range(g, g + SB_GROUP) for g in range(0, N_HEADS, SB_GROUP)]
        assert len(groups) == 4
        sbs = [{} for _ in groups]
        S, X, V = sb_scores, sb_suffix, sb_values
        W, N, O = mlstm_weights, mlstm_numerators, mlstm_outputs
        order = [(S, 0), (W,), (X, 0), (S, 1), (V, 0), (X, 1), (S, 2), (N,),
                 (V, 1), (X, 2), (S, 3), (V, 2), (X, 3), (O,), (V, 3)]
        for fn, *group in order:
            if group:
                fn(groups[group[0]], sbs[group[0]])
            else:
                fn(st)
        for hs, sb in zip(groups, sbs):
            for h in hs:
                write_out(h, sb["pv"][h])

        def cond(state):
            j, alive_h = state
            return jnp.logical_and(j >= 0, alive_h > 0)

        for h in heads:
            def body(state, h=h):
                j, _ = state
                z = scores_z(h, keys(h, j, 1))
                nlk = softplus2(z)
                carry = carry_ref[h][:, 0:1]
                a = jnp.exp2(z - nlk + suffix(nlk) + carry)
                acc = acc_ref[h] + jnp.dot(a.astype(BF16), values(h, j, 1), preferred_element_type=F32)
                acc_ref[h] = acc
                write_out(h, acc)
                carry = carry + total(nlk)
                carry_ref[h] = jnp.broadcast_to(carry, (tq, LANES))
                return j - 1, (jnp.max(carry) > UNDERFLOW_LOG2).astype(jnp.int32)

            lax.while_loop(cond, body, (c - 2, alive[h]))


def _mixers(p, gate_a, gates, gates_t, conv_w, bif_row, bif_col, norm_g, csm, B, S, L):
    T = B * S
    D = N_HEADS * HEAD_DIM
    nc = S // L
    tok = lambda blk: pl.BlockSpec((L, D), lambda b, c: (b * nc + c, blk))
    seq = lambda blk: pl.BlockSpec((S, D), lambda b, c: (b, blk))
    const = lambda shape: pl.BlockSpec(shape, lambda b, c: (0,) * len(shape))
    return pl.pallas_call(
        functools.partial(_mixers_kernel, L=L),
        grid=(B, nc),
        in_specs=[
            tok(PW_QK_A), tok(PW_V_A), tok(0),
            pl.BlockSpec((L, LANES), lambda b, c: (b * nc + c, 0)),
            pl.BlockSpec((1, 2 * N_HEADS, L), lambda b, c: (b * nc + c, 0, 0)),
            tok(PW_Q_B), seq(PW_K_B), seq(PW_V_B), tok(PW_GATE_B),
            const((CONV_K, D)), const((1, LANES)), const((2 * N_HEADS, 1)), const((1, D)),
            const((L, L)),
        ],
        out_specs=pl.BlockSpec((L, D), lambda b, c: (b * nc + c, 0)),
        out_shape=jax.ShapeDtypeStruct((T, D), BF16),
        scratch_shapes=[
            pltpu.VMEM((L + CONV_HALO, D), F32),
            pltpu.VMEM((N_HEADS, LANES, 2 * HEAD_DIM), F32),
            pltpu.VMEM((1, LANES), F32),
            pltpu.VMEM((L, D), F32),
            pltpu.VMEM((N_HEADS, L, HEAD_DIM), F32),
            pltpu.VMEM((N_HEADS, L, LANES), F32),
        ],
        compiler_params=pltpu.CompilerParams(
            dimension_semantics=("parallel", "arbitrary"), vmem_limit_bytes=VMEM_LIMIT_MIXERS),
        name="mixers",
    )(p, p, gate_a, gates, gates_t, p, p, p, p, conv_w, bif_row, bif_col, norm_g, csm)


def _mlp_kernel(x_ref, y_ref, wo_ref, g_ref, wu_ref, wd_ref, out_ref, *, tf):
    x1 = x_ref[...] + jnp.dot(y_ref[...], wo_ref[0], preferred_element_type=F32)
    ms = jnp.mean(x1 * x1, axis=-1, keepdims=True)
    h = (x1 * lax.rsqrt(ms + EPS) * g_ref[...]).astype(BF16)
    out_ref[...] = x1
    for f in range(wu_ref.shape[2] // tf):
        u = jnp.maximum(jnp.dot(h, wu_ref[0, :, f * tf:(f + 1) * tf], preferred_element_type=F32), 0.0)
        out_ref[...] += jnp.dot((u * u).astype(BF16), wd_ref[0, f * tf:(f + 1) * tf, :], preferred_element_type=F32)


def _outproj_mlp(x2d, y, w_out, g, w_up, w_down, layer, tm, tf):
    T, D = x2d.shape
    F = w_up.shape[2]
    resident = lambda shape: pl.BlockSpec((1,) + shape, lambda i: (layer, 0, 0), pipeline_mode=pl.Buffered(1))
    return pl.pallas_call(
        functools.partial(_mlp_kernel, tf=tf),
        grid=(T // tm,),
        in_specs=[
            pl.BlockSpec((tm, D), lambda i: (i, 0)),
            pl.BlockSpec((tm, D), lambda i: (i, 0)),
            resident((D, D)),
            pl.BlockSpec((1, D), lambda i: (0, 0)),
            resident((D, F)),
            resident((F, D)),
        ],
        out_specs=pl.BlockSpec((tm, D), lambda i: (i, 0)),
        out_shape=jax.ShapeDtypeStruct((T, D), F32),
        compiler_params=pltpu.CompilerParams(
            dimension_semantics=("parallel",), vmem_limit_bytes=VMEM_LIMIT),
        name="outproj_mlp",
    )(x2d, y, w_out, g, w_up, w_down)


def _tiles(B, S):
    T = B * S
    return dict(
        tm_in=min(512, T),
        chunk=min(256, S),
        tm_mlp=min(1024, T),
        tf=1024,
    )


def kernel(x, norm_mix_g, w_in, b_if, b_gate, conv_w, mlstm_norm_g, sb_q_norm_g, sb_k_norm_g,
           w_out, norm_mlp_g, w_up, w_down):
    B, S, D = x.shape
    depth = w_in.shape[0]
    T = B * S
    assert D == N_HEADS * HEAD_DIM and w_in.shape[2] == 8 * D + 2 * N_HEADS
    cfg = _tiles(B, S)
    tq = cfg["chunk"]

    r = lax.broadcasted_iota(jnp.int32, (tq, tq), 0)
    c = lax.broadcasted_iota(jnp.int32, (tq, tq), 1)
    csm = jnp.where(r > c, -1.0, 0.0).astype(BF16)

    w_main_all, w_gate_all = _wprep(jnp.swapaxes(w_in, 1, 2))

    w_out_b, w_up_b, w_down_b = w_out.astype(BF16), w_up.astype(BF16), w_down.astype(BF16)

    x2d = x.reshape(T, D)
    for l in range(depth):
        qk_gain = jnp.concatenate([jnp.tile(sb_q_norm_g[l], N_HEADS),
                                   jnp.tile(sb_k_norm_g[l] * (HEAD_DIM ** -0.5 * LOG2_E), N_HEADS)]
                                  ).reshape(1, 2 * D)
        p, gate_a, gates, gates_t = _inproj(
            x2d, norm_mix_g[l].reshape(1, D), w_main_all, w_gate_all, l, qk_gain,
            b_gate[l][:D].reshape(1, D), b_gate[l][D:].reshape(1, D), cfg["tm_in"], cfg["chunk"])
        bif_row = jnp.pad(b_if[l], (0, LANES - 2 * N_HEADS)).reshape(1, LANES)
        bif_col = b_if[l].reshape(2 * N_HEADS, 1)
        L = cfg["chunk"]
        y = _mixers(p, gate_a, gates, gates_t, conv_w[l], bif_row, bif_col, mlstm_norm_g[l].reshape(1, D),
                    csm, B, S, L)
        x2d = _outproj_mlp(x2d, y, w_out_b, norm_mlp_g[l].reshape(1, D), w_up_b, w_down_b, l,
                           cfg["tm_mlp"], cfg["tf"])
    return x2d.reshape(B, S, D)
```

```python
import functools

import jax
import jax.numpy as jnp
from jax import lax
from jax.experimental import pallas as pl
from jax.experimental.pallas import tpu as pltpu

F32 = jnp.float32
BF16 = jnp.bfloat16

EPS = 1e-6
N_HEADS = 8
HEAD_DIM = 128
DQK_A = 64
CONV_K = 4
LANES = 128
CONV_HALO = 8
VMEM_LIMIT = 56 * 1024 * 1024
VMEM_LIMIT_MIXERS = 58 * 1024 * 1024
LOG2_E = 1.4426950408889634
UNDERFLOW_LOG2 = -110.0 * LOG2_E

W_IN_BLOCKS = (0, 1, 2, 6, 7, 5, 3, 4)
N_BLK = 8
INPROJ_TN = 2
SB_GROUP = 2
PW_QK_A, PW_V_A, PW_GATE_B, PW_V_B, PW_Q_B, PW_K_B = range(6)


def _log_sigmoid(x):
    return jnp.minimum(x, 0.0) - jnp.log(1.0 + jnp.exp2(jnp.abs(x) * -LOG2_E))


def _exp_neg(x):
    return jnp.exp2(x * -LOG2_E)


def _sigmoid(x):
    return 1.0 / (1.0 + _exp_neg(x))


def _wprep_kernel(w_ref, wg_ref, main_ref, gate_ref):
    main_ref[0] = w_ref[0].astype(main_ref.dtype)
    g = wg_ref[0]
    row = lax.broadcasted_iota(jnp.int32, g.shape, 0)
    gate_ref[0] = jnp.where(row < 2 * N_HEADS, g, 0.0).astype(gate_ref.dtype)


def _wprep(w_in_t):
    depth, n_in, D = w_in_t.shape
    n_gate = 2 * N_HEADS
    gate_lo = 3 * D

    def block_row(b):
        pos = functools.reduce(lambda acc, kv: jnp.where(b == kv[0], kv[1], acc), enumerate(W_IN_BLOCKS), 0)
        return pl.multiple_of(pos * D + jnp.where(pos >= 3, n_gate, 0), n_gate)

    return pl.pallas_call(
        _wprep_kernel,
        grid=(depth, N_BLK),
        in_specs=[pl.BlockSpec((pl.Element(1), pl.Element(D), pl.Element(D)), lambda l, b: (l, block_row(b), 0)),
                  pl.BlockSpec((pl.Element(1), pl.Element(LANES), pl.Element(D)), lambda l, b: (l, gate_lo, 0))],
        out_specs=[pl.BlockSpec((1, D, D), lambda l, b: (l, b, 0)),
                   pl.BlockSpec((1, LANES, D), lambda l, b: (l, 0, 0))],
        out_shape=[jax.ShapeDtypeStruct((depth, N_BLK * D, D), BF16),
                   jax.ShapeDtypeStruct((depth, LANES, D), BF16)],
        compiler_params=pltpu.CompilerParams(
            dimension_semantics=("parallel", "arbitrary"), vmem_limit_bytes=VMEM_LIMIT),
        name="inproj_weights",
    )(w_in_t, w_in_t)


def _dot_nt(a, b_t):
    return lax.dot_general(a, b_t, (((1,), (1,)), ((), ())), preferred_element_type=F32)


def _inproj_kernel(x0_ref, xn_ref, g_ref, w_ref, wg_ref, qkg_ref, bga_ref, bgb_ref,
                   p_ref, ga_ref, gates_ref, gates_t_ref, h_ref, *, chunk):
    D = xn_ref.shape[1]
    tn = INPROJ_TN * D

    def rms(x):
        return (x * lax.rsqrt(jnp.mean(x * x, axis=-1, keepdims=True) + EPS) * g_ref[...]).astype(BF16)

    @pl.when(pl.program_id(0) == 0)
    def _():
        h_ref[...] = rms(x0_ref[...])

    h = h_ref[...]
    gates = _dot_nt(h, wg_ref[0])
    gates_ref[...] = gates
    gates_t = gates.T[:2 * N_HEADS, :]
    for k in range(gates_t_ref.shape[0]):
        gates_t_ref[k] = gates_t[:, k * chunk:(k + 1) * chunk]

    acc = _dot_nt(h, w_ref[0, 3 * tn:4 * tn, :])
    gain = qkg_ref[...]
    for hd in range(2 * N_HEADS):
        sl = slice(hd * HEAD_DIM, (hd + 1) * HEAD_DIM)
        a = acc[:, sl]
        ms = jnp.mean(a * a, axis=-1, keepdims=True)
        p_ref[:, 2 * tn + hd * HEAD_DIM:2 * tn + (hd + 1) * HEAD_DIM] = (
            a * lax.rsqrt(ms + EPS) * gain[:, sl]).astype(p_ref.dtype)

    acc = _dot_nt(h, w_ref[0, 1 * tn:2 * tn, :])
    gate = 1.0 / ((1.0 + _exp_neg(acc[:, :D])) * (1.0 + _exp_neg(acc[:, D:] + bga_ref[...])))
    ga_ref[...] = gate.astype(ga_ref.dtype)

    acc = _dot_nt(h, w_ref[0, 2 * tn:3 * tn, :])
    p_ref[:, tn:tn + D] = _sigmoid(acc[:, :D] + bgb_ref[...]).astype(p_ref.dtype)
    p_ref[:, tn + D:2 * tn] = acc[:, D:].astype(p_ref.dtype)

    h_next = rms(xn_ref[...])

    acc = _dot_nt(h, w_ref[0, 0 * tn:1 * tn, :])
    p_ref[:, 0:tn] = acc.astype(p_ref.dtype)
    h_ref[...] = h_next


def _inproj(x2d, g, w_main, w_gate, layer, qk_gain, b_gate_a, b_gate_b, tm, chunk):
    T, D = x2d.shape
    tn = INPROJ_TN * D
    n_rows = w_main.shape[1]
    assert n_rows == 4 * tn and qk_gain.shape == (1, tn) and tm % chunk == 0
    const = lambda shape: pl.BlockSpec(shape, lambda i: (0,) * len(shape))
    return pl.pallas_call(
        functools.partial(_inproj_kernel, chunk=chunk),
        grid=(T // tm,),
        in_specs=[
            pl.BlockSpec((tm, D), lambda i: (0, 0), pipeline_mode=pl.Buffered(1)),
            pl.BlockSpec((tm, D), lambda i: (jnp.minimum(i + 1, T // tm - 1), 0)),
            const((1, D)),
            pl.BlockSpec((1, n_rows, D), lambda i: (layer, 0, 0), pipeline_mode=pl.Buffered(1)),
            pl.BlockSpec((1, LANES, D), lambda i: (layer, 0, 0), pipeline_mode=pl.Buffered(1)),
            const((1, tn)), const((1, D)), const((1, D)),
        ],
        out_specs=[
            pl.BlockSpec((tm, 3 * tn), lambda i: (i, 0)),
            pl.BlockSpec((tm, D), lambda i: (i, 0)),
            pl.BlockSpec((tm, LANES), lambda i: (i, 0)),
            pl.BlockSpec((tm // chunk, 2 * N_HEADS, chunk), lambda i: (i, 0, 0)),
        ],
        out_shape=[
            jax.ShapeDtypeStruct((T, 3 * tn), BF16),
            jax.ShapeDtypeStruct((T, D), BF16),
            jax.ShapeDtypeStruct((T, LANES), F32),
            jax.ShapeDtypeStruct((T // chunk, 2 * N_HEADS, chunk), F32),
        ],
        scratch_shapes=[pltpu.VMEM((tm, D), BF16)],
        compiler_params=pltpu.CompilerParams(
            dimension_semantics=("arbitrary",), vmem_limit_bytes=VMEM_LIMIT),
        name="norm_inproj",
    )(x2d, x2d, g, w_main, w_gate, qk_gain, b_gate_a, b_gate_b)


def _mixers_kernel(qk_ref, va_ref, ga_ref, gates_ref, gates_t_ref, q_ref, k_ref, v_ref, gb_ref,
                   convw_ref, bif_row_ref, bif_col_ref, ng_ref, csm_ref,
                   out_ref, xbuf_ref, c_ref, m_ref, ya_ref, acc_ref, carry_ref, *, L):
    c = pl.program_id(1)
    d_qk = N_HEADS * DQK_A
    heads = range(N_HEADS)
    sl = [slice(h * HEAD_DIM, (h + 1) * HEAD_DIM) for h in heads]
    tq = L
    csm = csm_ref[...]

    def keys(h, j, n):
        return k_ref[pl.ds(pl.multiple_of(j * tq, tq), n * tq), sl[h]]

    def values(h, j, n):
        return v_ref[pl.ds(pl.multiple_of(j * tq, tq), n * tq), sl[h]]

    def scores_z(h, kj):
        return lax.dot_general(q_ref[:, sl[h]], kj, (((1,), (1,)), ((), ())), preferred_element_type=F32)

    def softplus2(z):
        return jnp.maximum(z, 0.0) + jnp.log2(1.0 + jnp.exp2(-jnp.abs(z)))

    def suffix(nlk):
        return jnp.dot(nlk.astype(BF16), csm, preferred_element_type=F32)

    def total(nlk):
        return -jnp.sum(nlk, axis=1, keepdims=True)

    def strict_mask():
        row = lax.broadcasted_iota(jnp.int32, (tq, tq), 0)
        col = lax.broadcasted_iota(jnp.int32, (tq, tq), 1)
        return col < row

    def head_col(x, h):
        return x[:, N_HEADS + h:N_HEADS + h + 1]

    def write_out(h, acc):
        out_ref[:, sl[h]] = (ya_ref[:, sl[h]] + gb_ref[:, sl[h]].astype(F32) * acc).astype(out_ref.dtype)

    def mlstm_prepare(st):
        xbuf_ref[CONV_HALO:CONV_HALO + L, :] = qk_ref[...].astype(F32)
        w = convw_ref[...]
        y = w[CONV_K - 1:CONV_K, :] * xbuf_ref[CONV_HALO:CONV_HALO + L, :]
        for tap in range(CONV_K - 1):
            off = CONV_HALO - (CONV_K - 1) + tap
            y = y + w[tap:tap + 1, :] * xbuf_ref[off:off + L, :]
        xbuf_ref[0:CONV_HALO, :] = xbuf_ref[L:L + CONV_HALO, :]
        qk = y * _sigmoid(y)

        row = lax.broadcasted_iota(jnp.int32, (L, L), 0)
        col = lax.broadcasted_iota(jnp.int32, (L, L), 1)
        st["causal"] = causal = col <= row
        tri = jnp.where(causal, 1.0, 0.0).astype(F32)
        pre_col = gates_ref[...] + bif_row_ref[...]
        b_col_all = jnp.dot(tri, _log_sigmoid(pre_col), preferred_element_type=F32,
                            precision=lax.Precision.HIGHEST)
        st["pre_row"] = pre_row = gates_t_ref[0] + bif_col_ref[...]
        st["b_row_all"] = lax.dot_general(_log_sigmoid(pre_row), tri, (((1,), (1,)), ((), ())),
                                          preferred_element_type=F32, precision=lax.Precision.HIGHEST)

        lane = lax.broadcasted_iota(jnp.int32, (L, LANES), 1)
        head_lanes = jnp.logical_and(lane >= N_HEADS, lane < 2 * N_HEADS)
        b_all = jnp.where(head_lanes, b_col_all, 0.0)
        i_all = jnp.where(head_lanes, pltpu.roll(pre_col, N_HEADS, axis=1), 0.0)
        rowi = lax.broadcasted_iota(jnp.int32, (L, LANES), 0)
        g_max = i_all - b_all
        shift = 1
        while shift < L:
            g_max = jnp.maximum(g_max, jnp.where(rowi >= shift, pltpu.roll(g_max, shift, axis=0), -jnp.inf))
            shift *= 2
        m_prev = m_ref[...]
        st["m_rel"] = m_rel = jnp.maximum(m_prev, g_max)
        st["inter_all"] = jnp.exp(m_prev - m_rel)
        st["floor_all"] = jnp.exp(-(b_all + m_rel))
        b_last = b_all[L - 1:L, :]
        a_all = b_last - b_all + i_all
        m_loc = jnp.max(a_all, axis=0, keepdims=True)
        w_all = jnp.exp(a_all - m_loc)
        m_new = jnp.maximum(b_last + m_prev, m_loc)
        st["sp_all"] = jnp.exp(b_last + m_prev - m_new)
        st["sc_all"] = jnp.exp(m_loc - m_new)
        m_ref[...] = m_new

        q_pair, km_b, vh, c0, s_qk, q_c, c_loc = ({} for _ in range(7))
        for h in heads:
            pair, half = divmod(h, 2)
            q_pair[h] = qk[:, pair * LANES:(pair + 1) * LANES].astype(BF16)
            k_pair = qk[:, d_qk + pair * LANES:d_qk + (pair + 1) * LANES]
            in_head = jnp.logical_and(lane >= half * DQK_A, lane < (half + 1) * DQK_A)
            km_b[h] = jnp.where(in_head, k_pair * (DQK_A ** -0.5), 0.0).astype(BF16)
            vh[h] = va_ref[:, sl[h]]
            c0[h] = c_ref[h]
        for h in heads:
            s_qk[h] = lax.dot_general(q_pair[h], km_b[h], (((1,), (1,)), ((), ())), preferred_element_type=F32)
        for h in heads:
            q_c[h] = jnp.dot(q_pair[h], c0[h].astype(BF16), preferred_element_type=F32)
        for h in heads:
            w_t = head_col(w_all, h)
            wv = jnp.concatenate([w_t * vh[h].astype(F32), jnp.broadcast_to(w_t, (L, HEAD_DIM))],
                                 axis=1).astype(BF16)
            c_loc[h] = lax.dot_general(km_b[h], wv, (((0,), (0,)), ((), ())), preferred_element_type=F32)
        st.update(vh=vh, c0=c0, s_qk=s_qk, q_c=q_c, c_loc=c_loc)

    def mlstm_weights(st):
        p = {}
        for h in heads:
            g_s = st["pre_row"][h:h + 1, :] - st["b_row_all"][N_HEADS + h:N_HEADS + h + 1, :]
            p[h] = jnp.exp(jnp.where(st["causal"], g_s - head_col(st["m_rel"], h), -jnp.inf)) * st["s_qk"][h]
        st["p"] = p

    def mlstm_numerators(st):
        ones_cols = jnp.ones((L, HEAD_DIM), BF16)
        st["p_v"] = {h: jnp.dot(st["p"][h].astype(BF16), jnp.concatenate([st["vh"][h], ones_cols], axis=1),
                                preferred_element_type=F32) for h in heads}

    def mlstm_outputs(st):
        mean_cols = jnp.full((HEAD_DIM, HEAD_DIM), 1.0 / HEAD_DIM, BF16)
        for h in heads:
            both = st["p_v"][h] + head_col(st["inter_all"], h) * st["q_c"][h]
            num, den = both[:, :HEAD_DIM], both[:, HEAD_DIM:]
            hh = num * (1.0 / jnp.maximum(jnp.abs(den), head_col(st["floor_all"], h)))
            ms = jnp.dot((hh * hh).astype(BF16), mean_cols, preferred_element_type=F32)
            hn = hh * lax.rsqrt(ms + EPS) * ng_ref[:, sl[h]]
            ya_ref[:, sl[h]] = ga_ref[:, sl[h]].astype(F32) * hn
        for h in heads:
            c_ref[h] = head_col(st["sp_all"], h) * st["c0"][h] + head_col(st["sc_all"], h) * st["c_loc"][h]

    @pl.when(c == 0)
    def _():
        xbuf_ref[0:CONV_HALO, :] = jnp.zeros((CONV_HALO, xbuf_ref.shape[1]), F32)
        c_ref[...] = jnp.zeros(c_ref.shape, F32)
        m_ref[...] = jnp.zeros(m_ref.shape, F32)
        st = {}
        mlstm_prepare(st)
        strict = strict_mask()
        z_all = {h: scores_z(h, keys(h, 0, 1)) for h in heads}
        mlstm_weights(st)
        d_all, nlk_all = {}, {}
        for h in heads:
            nlk = softplus2(z_all[h])
            d_all[h] = z_all[h] - nlk
            nlk_all[h] = jnp.where(strict, nlk, 0.0)
        cs = {h: suffix(nlk_all[h]) for h in heads}
        mlstm_numerators(st)
        a_all = {h: jnp.where(strict, jnp.exp2(d_all[h] + cs[h]), 0.0).astype(BF16) for h in heads}
        acc0 = {h: jnp.dot(a_all[h], values(h, 0, 1), preferred_element_type=F32) for h in heads}
        mlstm_outputs(st)
        for h in heads:
            write_out(h, acc0[h])

    @pl.when(c > 0)
    def _():
        st = {}
        mlstm_prepare(st)
        strict = strict_mask()
        alive = [None] * N_HEADS
        def sb_scores(hs, sb):
            sb["z"] = {h: scores_z(h, keys(h, c - 1, 2)) for h in hs}

        def sb_suffix(hs, sb):
            sb["d"], sb["nd"], sb["np"] = {}, {}, {}
            for h in hs:
                nlk = softplus2(sb["z"][h])
                sb["d"][h] = sb["z"][h] - nlk
                sb["nd"][h] = jnp.where(strict, nlk[:, tq:], 0.0)
                sb["np"][h] = nlk[:, :tq]
            sb["cd"] = {h: suffix(sb["nd"][h]) for h in hs}
            sb["cp"] = {h: suffix(sb["np"][h]) for h in hs}

        def sb_values(hs, sb):
            carry_d = {h: total(sb["nd"][h]) for h in hs}
            a_all = {}
            for h in hs:
                a_d = jnp.where(strict, jnp.exp2(sb["d"][h][:, tq:] + sb["cd"][h]), 0.0)
                a_p = jnp.exp2(sb["d"][h][:, :tq] + sb["cp"][h] + carry_d[h])
                a_all[h] = jnp.concatenate([a_p, a_d], axis=1).astype(BF16)
            pv = {h: jnp.dot(a_all[h], values(h, c - 1, 2), preferred_element_type=F32) for h in hs}
            for h in hs:
                acc_ref[h] = pv[h]
                sb.setdefault("pv", {})[h] = pv[h]
                carry = carry_d[h] + total(sb["np"][h])
                carry_ref[h] = jnp.broadcast_to(carry, (tq, LANES))
                alive[h] = (jnp.max(carry) > UNDERFLOW_LOG2).astype(jnp.int32)

        groups = [range(g, g + SB_GROUP) for g in range(0, N_HEADS, SB_GROUP)]
        assert len(groups) == 4
        sbs = [{} for _ in groups]
        S, X, V = sb_scores, sb_suffix, sb_values
        W, N, O = mlstm_weights, mlstm_numerators, mlstm_outputs
        order = [(S, 0), (W,), (X, 0), (S, 1), (V, 0), (X, 1), (S, 2), (N,),
                 (V, 1), (X, 2), (S, 3), (V, 2), (X, 3), (O,), (V, 3)]
        for fn, *group in order:
            if group:
                fn(groups[group[0]], sbs[group[0]])
            else:
                fn(st)
        for hs, sb in zip(groups, sbs):
            for h in hs:
                write_out(h, sb["pv"][h])

        def cond(state):
            j, alive_h = state
            return jnp.logical_and(j >= 0, alive_h > 0)

        for h in heads:
            def body(state, h=h):
                j, _ = state
                z = scores_z(h, keys(h, j, 1))
                nlk = softplus2(z)
                carry = carry_ref[h][:, 0:1]
                a = jnp.exp2(z - nlk + suffix(nlk) + carry)
                acc = acc_ref[h] + jnp.dot(a.astype(BF16), values(h, j, 1), preferred_element_type=F32)
                acc_ref[h] = acc
                write_out(h, acc)
                carry = carry + total(nlk)
                carry_ref[h] = jnp.broadcast_to(carry, (tq, LANES))
                return j - 1, (jnp.max(carry) > UNDERFLOW_LOG2).astype(jnp.int32)

            lax.while_loop(cond, body, (c - 2, alive[h]))


def _mixers(p, gate_a, gates, gates_t, conv_w, bif_row, bif_col, norm_g, csm, B, S, L):
    T = B * S
    D = N_HEADS * HEAD_DIM
    nc = S // L
    tok = lambda blk: pl.BlockSpec((L, D), lambda b, c: (b * nc + c, blk))
    seq = lambda blk: pl.BlockSpec((S, D), lambda b, c: (b, blk))
    const = lambda shape: pl.BlockSpec(shape, lambda b, c: (0,) * len(shape))
    return pl.pallas_call(
        functools.partial(_mixers_kernel, L=L),
        grid=(B, nc),
        in_specs=[
            tok(PW_QK_A), tok(PW_V_A), tok(0),
            pl.BlockSpec((L, LANES), lambda b, c: (b * nc + c, 0)),
            pl.BlockSpec((1, 2 * N_HEADS, L), lambda b, c: (b * nc + c, 0, 0)),
            tok(PW_Q_B), seq(PW_K_B), seq(PW_V_B), tok(PW_GATE_B),
            const((CONV_K, D)), const((1, LANES)), const((2 * N_HEADS, 1)), const((1, D)),
            const((L, L)),
        ],
        out_specs=pl.BlockSpec((L, D), lambda b, c: (b * nc + c, 0)),
        out_shape=jax.ShapeDtypeStruct((T, D), BF16),
        scratch_shapes=[
            pltpu.VMEM((L + CONV_HALO, D), F32),
            pltpu.VMEM((N_HEADS, LANES, 2 * HEAD_DIM), F32),
            pltpu.VMEM((1, LANES), F32),
            pltpu.VMEM((L, D), F32),
            pltpu.VMEM((N_HEADS, L, HEAD_DIM), F32),
            pltpu.VMEM((N_HEADS, L, LANES), F32),
        ],
        compiler_params=pltpu.CompilerParams(
            dimension_semantics=("parallel", "arbitrary"), vmem_limit_bytes=VMEM_LIMIT_MIXERS),
        name="mixers",
    )(p, p, gate_a, gates, gates_t, p, p, p, p, conv_w, bif_row, bif_col, norm_g, csm)


def _mlp_kernel(x_ref, y_ref, wo_ref, g_ref, wu_ref, wd_ref, out_ref, *, tf):
    x1 = x_ref[...] + jnp.dot(y_ref[...], wo_ref[0], preferred_element_type=F32)
    ms = jnp.mean(x1 * x1, axis=-1, keepdims=True)
    h = (x1 * lax.rsqrt(ms + EPS) * g_ref[...]).astype(BF16)
    out_ref[...] = x1
    for f in range(wu_ref.shape[2] // tf):
        u = jnp.maximum(jnp.dot(h, wu_ref[0, :, f * tf:(f + 1) * tf], preferred_element_type=F32), 0.0)
        out_ref[...] += jnp.dot((u * u).astype(BF16), wd_ref[0, f * tf:(f + 1) * tf, :], preferred_element_type=F32)


def _outproj_mlp(x2d, y, w_out, g, w_up, w_down, layer, tm, tf):
    T, D = x2d.shape
    F = w_up.shape[2]
    resident = lambda shape: pl.BlockSpec((1,) + shape, lambda i: (layer, 0, 0), pipeline_mode=pl.Buffered(1))
    return pl.pallas_call(
        functools.partial(_mlp_kernel, tf=tf),
        grid=(T // tm,),
        in_specs=[
            pl.BlockSpec((tm, D), lambda i: (i, 0)),
            pl.BlockSpec((tm, D), lambda i: (i, 0)),
            resident((D, D)),
            pl.BlockSpec((1, D), lambda i: (0, 0)),
            resident((D, F)),
            resident((F, D)),
        ],
        out_specs=pl.BlockSpec((tm, D), lambda i: (i, 0)),
        out_shape=jax.ShapeDtypeStruct((T, D), F32),
        compiler_params=pltpu.CompilerParams(
            dimension_semantics=("parallel",), vmem_limit_bytes=VMEM_LIMIT),
        name="outproj_mlp",
    )(x2d, y, w_out, g, w_up, w_down)


def _tiles(B, S):
    T = B * S
    return dict(
        tm_in=min(512, T),
        chunk=min(256, S),
        tm_mlp=min(1024, T),
        tf=1024,
    )


def kernel(x, norm_mix_g, w_in, b_if, b_gate, conv_w, mlstm_norm_g, sb_q_norm_g, sb_k_norm_g,
           w_out, norm_mlp_g, w_up, w_down):
    B, S, D = x.shape
    depth = w_in.shape[0]
    T = B * S
    assert D == N_HEADS * HEAD_DIM and w_in.shape[2] == 8 * D + 2 * N_HEADS
    cfg = _tiles(B, S)
    tq = cfg["chunk"]

    r = lax.broadcasted_iota(jnp.int32, (tq, tq), 0)
    c = lax.broadcasted_iota(jnp.int32, (tq, tq), 1)
    csm = jnp.where(r > c, -1.0, 0.0).astype(BF16)

    w_main_all, w_gate_all = _wprep(jnp.swapaxes(w_in, 1, 2))

    w_out_b, w_up_b, w_down_b = w_out.astype(BF16), w_up.astype(BF16), w_down.astype(BF16)

    x2d = x.reshape(T, D)
    for l in range(depth):
        qk_gain = jnp.concatenate([jnp.tile(sb_q_norm_g[l], N_HEADS),
                                   jnp.tile(sb_k_norm_g[l] * (HEAD_DIM ** -0.5 * LOG2_E), N_HEADS)]
                                  ).reshape(1, 2 * D)
        p, gate_a, gates, gates_t = _inproj(
            x2d, norm_mix_g[l].reshape(1, D), w_main_all, w_gate_all, l, qk_gain,
            b_gate[l][:D].reshape(1, D), b_gate[l][D:].reshape(1, D), cfg["tm_in"], cfg["chunk"])
        bif_row = jnp.pad(b_if[l], (0, LANES - 2 * N_HEADS)).reshape(1, LANES)
        bif_col = b_if[l].reshape(2 * N_HEADS, 1)
        L = cfg["chunk"]
        y = _mixers(p, gate_a, gates, gates_t, conv_w[l], bif_row, bif_col, mlstm_norm_g[l].reshape(1, D),
                    csm, B, S, L)
        x2d = _outproj_mlp(x2d, y, w_out_b, norm_mlp_g[l].reshape(1, D), w_up_b, w_down_b, l,
                           cfg["tm_mlp"], cfg["tf"])
    return x2d.reshape(B, S, D)
```
